```python
import jax, jax.numpy as jnp
from jax import lax
import numpy as np

D_MODEL = 4096
BATCH = 1
SEQ = 8192
DEPTH = 1

HEAD_DIM = 128
N_HEADS_ATTN = 16
ATTN_WIDTH = N_HEADS_ATTN * HEAD_DIM
DILATED_PATTERNS = ((128, 1), (512, 4), (2048, 16))
LRU_WIDTH = D_MODEL
LRU_BLOCK = 256
N_LRU_BLOCKS = LRU_WIDTH // LRU_BLOCK
CONV_WIDTH = 4
CONV_PAD = (2, 1)
LRU_C = 8.0
D_FF = 4 * D_MODEL
LN_EPS = 1e-5
DEEPNORM_ALPHA = (2.0 * DEPTH) ** 0.25
DEEPNORM_BETA = (8.0 * DEPTH) ** -0.25
IN_SIZES = (ATTN_WIDTH, ATTN_WIDTH, ATTN_WIDTH, LRU_WIDTH, D_MODEL, D_MODEL)
IN_SPLITS = tuple(int(s) for s in np.cumsum(IN_SIZES)[:-1])

kernel_name = "hybrid_dilated_attn_rglru_deepnorm_block"


def layer_norm(x, g, b):
    xf = x.astype(jnp.float32)
    mu = jnp.mean(xf, axis=-1, keepdims=True)
    var = jnp.mean(jnp.square(xf - mu), axis=-1, keepdims=True)
    return ((xf - mu) * lax.rsqrt(var + LN_EPS) * g.astype(jnp.float32) + b.astype(jnp.float32)).astype(x.dtype)


def alibi_slopes(n_heads):
    return 2.0 ** (-8.0 * jnp.arange(1, n_heads + 1, dtype=jnp.float32) / n_heads)


def banded_attention(q, k, v, half, dist_scale, slopes):
    B, H, R, L, hd = q.shape
    blk = half
    nb = -(-L // blk)
    pad = nb * blk - L
    qb = jnp.pad(q, ((0, 0), (0, 0), (0, 0), (0, pad), (0, 0))).reshape(B, H, R, nb, blk, hd)

    def kv_blocks(t):
        tp = jnp.pad(t, ((0, 0), (0, 0), (0, 0), (blk, pad + blk), (0, 0))).reshape(B, H, R, nb + 2, blk, hd)
        return jnp.concatenate([tp[:, :, :, :-2], tp[:, :, :, 1:-1], tp[:, :, :, 2:]], axis=-2)

    kb, vb = kv_blocks(k), kv_blocks(v)
    rel = jnp.arange(3 * blk)[None, :] - blk - jnp.arange(blk)[:, None]
    key_pos = jnp.arange(nb)[:, None] * blk - blk + jnp.arange(3 * blk)[None, :]
    valid = (jnp.abs(rel) <= half)[None] & ((key_pos >= 0) & (key_pos < L))[:, None, :]
    dist = dist_scale * jnp.abs(rel).astype(jnp.float32)
    s = jnp.einsum('bhrnqd,bhrnkd->bhrnqk', qb, kb) * (HEAD_DIM ** -0.5)
    s = s - slopes[:, None, None, None, None] * dist
    s = jnp.where(valid, s, -jnp.inf)
    m = jnp.max(s, axis=-1, keepdims=True)
    e = jnp.exp(s - m)
    den = jnp.sum(e, axis=-1, keepdims=True)
    o = jnp.einsum('bhrnqk,bhrnkd->bhrnqd', e, vb) / den
    lse = (m + jnp.log(den))[..., 0]
    o = o.reshape(B, H, R, nb * blk, hd)[:, :, :, :L]
    lse = lse.reshape(B, H, R, nb * blk)[:, :, :, :L]
    return o, lse


def dilated_attention(q, k, v):
    B, S, H, hd = q.shape
    slopes = alibi_slopes(H)
    outs, lses = [], []
    for window, dil in DILATED_PATTERNS:
        half = window // (2 * dil)

        def split(t):
            return t.astype(jnp.float32).reshape(B, S // dil, dil, H, hd).transpose(0, 3, 2, 1, 4)

        o, lse = banded_attention(split(q), split(k), split(v), half, float(dil), slopes)
        outs.append(o.transpose(0, 3, 2, 1, 4).reshape(B, S, H, hd))
        lses.append(lse.transpose(0, 3, 2, 1).reshape(B, S, H))
    w = jax.nn.softmax(jnp.stack(lses, axis=0), axis=0)
    o = jnp.sum(w[..., None] * jnp.stack(outs, axis=0), axis=0)
    return o.reshape(B, S, H * hd).astype(q.dtype)


def rg_lru(xc, w_a, b_a, w_x, b_x, lam, reverse):
    B, S, C = xc.shape
    xf = xc.astype(jnp.float32)
    xb = xf.reshape(B, S, N_LRU_BLOCKS, LRU_BLOCK)
    r = jax.nn.sigmoid(jnp.einsum('bsnc,ncd->bsnd', xb, w_a.astype(jnp.float32)).reshape(B, S, C) + b_a.astype(jnp.float32))
    i = jax.nn.sigmoid(jnp.einsum('bsnc,ncd->bsnd', xb, w_x.astype(jnp.float32)).reshape(B, S, C) + b_x.astype(jnp.float32))
    log_a = -LRU_C * r * jax.nn.softplus(-lam.astype(jnp.float32))
    a = jnp.exp(log_a)
    bterm = jnp.sqrt(-jnp.expm1(2.0 * log_a)) * (i * xf)

    def combine(c1, c2):
        a1, b1 = c1
        a2, b2 = c2
        return a1 * a2, a2 * b1 + b2

    _, h = lax.associative_scan(combine, (a, bterm), axis=1, reverse=reverse)
    return h


def setup_inputs(seed: int = 0) -> dict:
    key = jax.random.key(seed)
    ks = jax.random.split(key, 32)
    f32 = jnp.float32
    L, D = DEPTH, D_MODEL

    def nrm(k, shape, scale):
        return jax.random.normal(k, shape, f32) * scale

    def lam_init(k):
        u = jax.random.uniform(k, (L, LRU_WIDTH), f32, 0.9, 0.999)
        a = u ** (1.0 / LRU_C)
        return jnp.log(a) - jnp.log1p(-a)

    s_in = D ** -0.5
    bd = LRU_BLOCK ** -0.5
    w_in = jnp.concatenate([
        nrm(ks[1], (L, D, ATTN_WIDTH), s_in),
        nrm(ks[2], (L, D, ATTN_WIDTH), s_in),
        nrm(ks[3], (L, D, ATTN_WIDTH), s_in * DEEPNORM_BETA),
        nrm(ks[4], (L, D, LRU_WIDTH), s_in * DEEPNORM_BETA),
        nrm(ks[5], (L, D, 2 * D), s_in),
    ], axis=-1)
    return {
        "x": jax.random.normal(ks[0], (BATCH, SEQ, D), f32),
        "w_in": w_in,
        "gate_b": nrm(ks[6], (L, 2, D), 0.01),
        "conv_w": nrm(ks[7], (L, CONV_WIDTH, LRU_WIDTH), CONV_WIDTH ** -0.5),
        "conv_b": nrm(ks[8], (L, LRU_WIDTH), 0.01),
        "lru_wa_fwd": nrm(ks[9], (L, N_LRU_BLOCKS, LRU_BLOCK, LRU_BLOCK), bd),
        "lru_ba_fwd": nrm(ks[10], (L, LRU_WIDTH), 0.01),
        "lru_wx_fwd": nrm(ks[11], (L, N_LRU_BLOCKS, LRU_BLOCK, LRU_BLOCK), bd),
        "lru_bx_fwd": nrm(ks[12], (L, LRU_WIDTH), 0.01),
        "lru_lam_fwd": lam_init(ks[13]),
        "lru_wa_bwd": nrm(ks[14], (L, N_LRU_BLOCKS, LRU_BLOCK, LRU_BLOCK), bd),
        "lru_ba_bwd": nrm(ks[15], (L, LRU_WIDTH), 0.01),
        "lru_wx_bwd": nrm(ks[16], (L, N_LRU_BLOCKS, LRU_BLOCK, LRU_BLOCK), bd),
        "lru_bx_bwd": nrm(ks[17], (L, LRU_WIDTH), 0.01),
        "lru_lam_bwd": lam_init(ks[18]),
        "p_attn": nrm(ks[19], (L, ATTN_WIDTH, D), ATTN_WIDTH ** -0.5 * DEEPNORM_BETA),
        "p_lru": nrm(ks[20], (L, LRU_WIDTH, D), LRU_WIDTH ** -0.5 * DEEPNORM_BETA),
        "w_out": nrm(ks[21], (L, D, D), D ** -0.5 * DEEPNORM_BETA),
        "ln1_g": 1.0 + nrm(ks[22], (L, D), 0.01),
        "ln1_b": nrm(ks[23], (L, D), 0.01),
        "w_mlp1": nrm(ks[24], (L, D, D_FF), D ** -0.5 * DEEPNORM_BETA),
        "b_mlp1": nrm(ks[25], (L, D_FF), 0.01),
        "w_mlp2": nrm(ks[26], (L, D_FF, D), D_FF ** -0.5 * DEEPNORM_BETA),
        "b_mlp2": nrm(ks[27], (L, D), 0.01),
        "ln2_g": 1.0 + nrm(ks[28], (L, D), 0.01),
        "ln2_b": nrm(ks[29], (L, D), 0.01),
    }


def reference(x, w_in, gate_b, conv_w, conv_b,
              lru_wa_fwd, lru_ba_fwd, lru_wx_fwd, lru_bx_fwd, lru_lam_fwd,
              lru_wa_bwd, lru_ba_bwd, lru_wx_bwd, lru_bx_bwd, lru_lam_bwd,
              p_attn, p_lru, w_out, ln1_g, ln1_b,
              w_mlp1, b_mlp1, w_mlp2, b_mlp2, ln2_g, ln2_b):
    B, S, _ = x.shape
    for l in range(DEPTH):
        proj = x @ w_in[l]
        q, k, v, xr, g_a, g_b = jnp.split(proj, IN_SPLITS, axis=-1)
        q = q.reshape(B, S, N_HEADS_ATTN, HEAD_DIM)
        k = k.reshape(B, S, N_HEADS_ATTN, HEAD_DIM)
        v = v.reshape(B, S, N_HEADS_ATTN, HEAD_DIM)
        attn = dilated_attention(q, k, v)

        xc = lax.conv_general_dilated(
            xr, conv_w[l][:, None, :], window_strides=(1,), padding=[CONV_PAD],
            dimension_numbers=('NWC', 'WIO', 'NWC'), feature_group_count=LRU_WIDTH) + conv_b[l]
        h = (rg_lru(xc, lru_wa_fwd[l], lru_ba_fwd[l], lru_wx_fwd[l], lru_bx_fwd[l], lru_lam_fwd[l], False)
             + rg_lru(xc, lru_wa_bwd[l], lru_ba_bwd[l], lru_wx_bwd[l], lru_bx_bwd[l], lru_lam_bwd[l], True))
        h = h.astype(x.dtype)

        merged = (jax.nn.sigmoid(g_a + gate_b[l, 0]) * (attn @ p_attn[l])
                  + jax.nn.sigmoid(g_b + gate_b[l, 1]) * (h @ p_lru[l]))
        mix = merged @ w_out[l]
        x = layer_norm(DEEPNORM_ALPHA * x + mix, ln1_g[l], ln1_b[l])

        hid = jnp.square(jax.nn.relu(x @ w_mlp1[l] + b_mlp1[l]))
        x = layer_norm(DEEPNORM_ALPHA * x + (hid @ w_mlp2[l] + b_mlp2[l]), ln2_g[l], ln2_b[l])
    return x
```

```python
import functools

import jax
import jax.numpy as jnp
from jax import lax
from jax.experimental import pallas as pl
from jax.experimental.pallas import tpu as pltpu

HEAD_DIM = 128
N_HEADS = 16
ATTN_WIDTH = N_HEADS * HEAD_DIM
DILATIONS = (1, 4, 16)
HALF = 64
LRU_BLOCK = 256
LRU_C = 8.0
LN_EPS = 1e-5
CONV_LEFT = 2

V7X_VMEM_LIMIT_BYTES = 56 * 1024 * 1024
SUBLANES = 8
LANES = 128

QBLK = 128
KWIN = QBLK + 2 * HALF


def _cparams(sem):
    return pltpu.CompilerParams(dimension_semantics=sem, vmem_limit_bytes=V7X_VMEM_LIMIT_BYTES)


def _mm_kernel(*refs, nk, n_extra, n_out, epilogue):
    a_ref, b_ref = refs[0], refs[1]
    extra = refs[2:2 + n_extra]
    outs = refs[2 + n_extra:2 + n_extra + n_out]
    prod = jnp.dot(a_ref[...], b_ref[...], preferred_element_type=jnp.float32)
    if nk == 1:
        epilogue(prod, extra, outs)
        return
    acc_ref = refs[-1]
    k = pl.program_id(2)

    @pl.when(k == 0)
    def _():
        acc_ref[...] = jnp.zeros_like(acc_ref)

    acc_ref[...] += prod

    @pl.when(k == nk - 1)
    def _():
        epilogue(acc_ref[...], extra, outs)


def _matmul(a, b, *, name, n_cols, col_off, epilogue, out_dtypes, rows=(), tiles=(), tm=1024, tn=1024, tk=1024):
    M, K = a.shape
    tm, tn, tk = min(tm, M), min(tn, n_cols), min(tk, K)
    while col_off % tn or any(r_off % tn for _, r_off in rows):
        tn //= 2
    assert M % tm == 0 and n_cols % tn == 0 and K % tk == 0 and tn % LANES == 0
    nk = K // tk
    off = col_off // tn
    in_specs = [pl.BlockSpec((tm, tk), lambda i, j, k: (i, k)),
                pl.BlockSpec((tk, tn), lambda i, j, k: (k, j + off))]
    for r, r_off in rows:
        ro = r_off // tn
        in_specs.append(pl.BlockSpec((1, tn), lambda i, j, k, ro=ro: (0, j + ro)))
    for _ in tiles:
        in_specs.append(pl.BlockSpec((tm, tn), lambda i, j, k: (i, j)))
    out_specs = [pl.BlockSpec((tm, tn), lambda i, j, k: (i, j)) for _ in out_dtypes]
    out_shape = [jax.ShapeDtypeStruct((M, n_cols), dt) for dt in out_dtypes]
    n_extra = len(rows) + len(tiles)
    scratch = [pltpu.VMEM((tm, tn), jnp.float32)] if nk > 1 else []
    res = pl.pallas_call(
        functools.partial(_mm_kernel, nk=nk, n_extra=n_extra, n_out=len(out_dtypes), epilogue=epilogue),
        grid=(M // tm, n_cols // tn, nk),
        in_specs=in_specs, out_specs=out_specs, out_shape=out_shape, scratch_shapes=scratch,
        compiler_params=_cparams(("parallel", "parallel", "arbitrary")), name=name,
    )(a, b, *[r for r, _ in rows], *tiles)
    return res


def _epi_plain(acc, extra, outs):
    outs[0][...] = acc.astype(outs[0].dtype)


def _sigmoid(x):
    return 0.5 * jnp.tanh(0.5 * x) + 0.5


def _epi_gate(acc, extra, outs):
    outs[0][...] = _sigmoid(acc + extra[0][...]).astype(outs[0].dtype)


def _epi_relu2(acc, extra, outs):
    h = jnp.maximum(acc + extra[0][...], 0.0)
    outs[0][...] = (h * h).astype(outs[0].dtype)


def _epi_residual(acc, extra, outs, *, alpha):
    outs[0][...] = alpha * extra[0][...] + acc


def _epi_residual_bias(acc, extra, outs, *, alpha):
    outs[0][...] = alpha * extra[1][...] + (acc + extra[0][...])


def _merge_kernel(at_ref, pa_ref, h_ref, pl_ref, ga_ref, gb_ref, o_ref, acc_a, acc_b, *, n1, nk):
    k = pl.program_id(2)

    @pl.when(k == 0)
    def _():
        acc_a[...] = jnp.zeros_like(acc_a)
        acc_b[...] = jnp.zeros_like(acc_b)

    @pl.when(k < n1)
    def _():
        acc_a[...] += jnp.dot(at_ref[...], pa_ref[...], preferred_element_type=jnp.float32)

    @pl.when(k >= n1)
    def _():
        acc_b[...] += jnp.dot(h_ref[...].astype(jnp.bfloat16), pl_ref[...], preferred_element_type=jnp.float32)

    @pl.when(k == nk - 1)
    def _():
        o_ref[...] = (ga_ref[...].astype(jnp.float32) * acc_a[...]
                      + gb_ref[...].astype(jnp.float32) * acc_b[...]).astype(o_ref.dtype)


def _merge(attn, p_attn, h, p_lru, gates, *, tm=1024, tn=1024, tk=1024):
    M, K1 = attn.shape
    K2 = h.shape[1]
    N = p_attn.shape[1]
    tm, tn, tk = min(tm, M), min(tn, N), min(tk, K1, K2)
    n1, n2 = K1 // tk, K2 // tk
    nk = n1 + n2
    gb_off = N // tn
    return pl.pallas_call(
        functools.partial(_merge_kernel, n1=n1, nk=nk),
        grid=(M // tm, N // tn, nk),
        in_specs=[
            pl.BlockSpec((tm, tk), lambda i, j, k: (i, jnp.minimum(k, n1 - 1))),
            pl.BlockSpec((tk, tn), lambda i, j, k: (jnp.minimum(k, n1 - 1), j)),
            pl.BlockSpec((tm, tk), lambda i, j, k: (i, jnp.maximum(k - n1, 0))),
            pl.BlockSpec((tk, tn), lambda i, j, k: (jnp.maximum(k - n1, 0), j)),
            pl.BlockSpec((tm, tn), lambda i, j, k: (i, j)),
            pl.BlockSpec((tm, tn), lambda i, j, k: (i, j + gb_off)),
        ],
        out_specs=pl.BlockSpec((tm, tn), lambda i, j, k: (i, j)),
        out_shape=jax.ShapeDtypeStruct((M, N), jnp.bfloat16),
        scratch_shapes=[pltpu.VMEM((tm, tn), jnp.float32), pltpu.VMEM((tm, tn), jnp.float32)],
        compiler_params=_cparams(("parallel", "parallel", "arbitrary")), name="merge",
    )(attn, p_attn, h, p_lru, gates, gates)


def _ln_kernel(u_ref, g_ref, b_ref, *outs):
    u = u_ref[...]
    mu = jnp.mean(u, axis=-1, keepdims=True)
    c = u - mu
    var = jnp.mean(c * c, axis=-1, keepdims=True)
    y = c * lax.rsqrt(var + LN_EPS) * g_ref[...] + b_ref[...]
    for o in outs:
        o[...] = y.astype(o.dtype)


def _layer_norm(u, g, b, out_dtypes, *, tr=256):
    M, D = u.shape
    tr = min(tr, M)
    return pl.pallas_call(
        _ln_kernel,
        grid=(M // tr,),
        in_specs=[pl.BlockSpec((tr, D), lambda i: (i, 0)),
                  pl.BlockSpec((1, D), lambda i: (0, 0)),
                  pl.BlockSpec((1, D), lambda i: (0, 0))],
        out_specs=[pl.BlockSpec((tr, D), lambda i: (i, 0)) for _ in out_dtypes],
        out_shape=[jax.ShapeDtypeStruct((M, D), dt) for dt in out_dtypes],
        compiler_params=_cparams(("parallel",)), name="layer_norm",
    )(u, g, b)


def _attn_kernel(slope_ref, q_ref, k_ref, v_ref, o_ref, lse_ref, *, seq, tqs, scale):
    h = pl.program_id(1)
    step = pl.program_id(2)
    neg_slope = -slope_ref[h]
    n_sub = tqs // QBLK
    col = lax.broadcasted_iota(jnp.int32, (QBLK, KWIN), 1)
    row = lax.broadcasted_iota(jnp.int32, (QBLK, KWIN), 0)
    delta = col - row
    eye = (lax.broadcasted_iota(jnp.int32, (QBLK, QBLK), 0)
           == lax.broadcasted_iota(jnp.int32, (QBLK, QBLK), 1))

    def bias_for(off):
        dist = jnp.abs(delta + off)
        return jnp.where(dist <= HALF, dist.astype(jnp.float32) * neg_slope, -jnp.inf)

    interior_bias = bias_for(-HALF)
    for jj in range(n_sub):
        q0 = step * tqs + jj * QBLK
        if jj == 0 or jj == n_sub - 1:
            kstart = jnp.clip(q0 - HALF, 0, seq - KWIN)
            bias = bias_for(kstart - q0)
        else:
            kstart = q0 - HALF
            bias = interior_bias
        kstart = pl.multiple_of(kstart, HALF)
        q = q_ref[pl.ds(jj * QBLK, QBLK), :]
        kw = k_ref[pl.ds(kstart, KWIN), :]
        vw = v_ref[pl.ds(kstart, KWIN), :]
        s = lax.dot_general(q, kw, (((1,), (1,)), ((), ())), preferred_element_type=jnp.float32)
        s = s * scale + bias
        m = jnp.max(s, axis=-1, keepdims=True)
        e = jnp.exp(s - m)
        den = jnp.sum(e, axis=-1, keepdims=True)
        o = jnp.dot(e.astype(jnp.bfloat16), vw, preferred_element_type=jnp.float32) / den
        o_ref[pl.ds(jj * QBLK, QBLK), :] = o.astype(o_ref.dtype)
        lse = m + jnp.log(den)
        lse_row = jnp.sum(jnp.where(eye, lse, 0.0), axis=0, keepdims=True)
        lse_ref[0, :, pl.ds(jj * QBLK, QBLK)] = lse_row


def _dilated_attention_pattern(qkv, slopes, dil):
    S = qkv.shape[0]
    L = S // dil
    tqs = min(L, 1024)
    blocks_per_res = 3 * N_HEADS
    view = qkv.reshape(L, dil * 3 * ATTN_WIDTH)
    o, lse = pl.pallas_call(
        functools.partial(_attn_kernel, seq=L, tqs=tqs, scale=HEAD_DIM ** -0.5),
        grid=(dil, N_HEADS, L // tqs),
        in_specs=[
            pl.BlockSpec(memory_space=pltpu.SMEM),
            pl.BlockSpec((tqs, HEAD_DIM), lambda r, h, i: (i, r * blocks_per_res + h)),
            pl.BlockSpec((L, HEAD_DIM), lambda r, h, i: (0, r * blocks_per_res + N_HEADS + h)),
            pl.BlockSpec((L, HEAD_DIM), lambda r, h, i: (0, r * blocks_per_res + 2 * N_HEADS + h)),
        ],
        out_specs=[
            pl.BlockSpec((tqs, HEAD_DIM), lambda r, h, i: (i, r * N_HEADS + h)),
            pl.BlockSpec((1, 1, tqs), lambda r, h, i: (r * N_HEADS + h, 0, i)),
        ],
        out_shape=[jax.ShapeDtypeStruct((L, dil * ATTN_WIDTH), jnp.float32),
                   jax.ShapeDtypeStruct((dil * N_HEADS, 1, L), jnp.float32)],
        compiler_params=_cparams(("parallel", "parallel", "arbitrary")), name=f"attn_dil{dil}",
    )(slopes * float(dil), view, view, view)
    o = o.reshape(S, ATTN_WIDTH)
    lse = lse.reshape(dil, N_HEADS, L).transpose(2, 0, 1).reshape(S, N_HEADS)
    return o, lse


def _combine_kernel(o1_ref, o2_ref, o3_ref, l1_ref, l2_ref, l3_ref, out_ref):
    l1, l2, l3 = l1_ref[...], l2_ref[...], l3_ref[...]
    m = jnp.maximum(jnp.maximum(l1, l2), l3)
    w1, w2, w3 = jnp.exp(l1 - m), jnp.exp(l2 - m), jnp.exp(l3 - m)
    z = w1 + w2 + w3
    expand = (lax.broadcasted_iota(jnp.int32, (N_HEADS, ATTN_WIDTH), 1) // HEAD_DIM
              == lax.broadcasted_iota(jnp.int32, (N_HEADS, ATTN_WIDTH), 0)).astype(jnp.float32)

    def spread(w):
        return jnp.dot(w / z, expand, preferred_element_type=jnp.float32, precision=lax.Precision.HIGHEST)

    out = spread(w1) * o1_ref[...] + spread(w2) * o2_ref[...] + spread(w3) * o3_ref[...]
    out_ref[...] = out.astype(out_ref.dtype)


def _combine(os_, lses, *, tr=256):
    S = os_[0].shape[0]
    tr = min(tr, S)
    return pl.pallas_call(
        _combine_kernel,
        grid=(S // tr,),
        in_specs=[pl.BlockSpec((tr, ATTN_WIDTH), lambda i: (i, 0))] * 3
        + [pl.BlockSpec((tr, N_HEADS), lambda i: (i, 0))] * 3,
        out_specs=pl.BlockSpec((tr, ATTN_WIDTH), lambda i: (i, 0)),
        out_shape=jax.ShapeDtypeStruct((S, ATTN_WIDTH), jnp.bfloat16),
        compiler_params=_cparams(("parallel",)), name="attn_combine",
    )(*os_, *lses)


def _lru_direction(x_ref, xp_ref, xn_ref, cw_ref, cb_ref, wa_ref, wx_ref, ba_ref, bx_ref, lam_ref,
                   carry_ref, *, tb, n_tb, reverse):
    T, C = x_ref.shape
    prev = jnp.where(tb > 0, xp_ref[...], 0.0)
    nxt = jnp.where(tb < n_tb - 1, xn_ref[...], 0.0)
    ext = jnp.concatenate([prev, x_ref[...], nxt], axis=0)
    base = SUBLANES - CONV_LEFT
    xc = cb_ref[...]
    for j in range(4):
        xc = xc + cw_ref[j:j + 1, :] * ext[base + j:base + j + T, :]
    xcb = xc.astype(jnp.bfloat16)
    ga, gx = [], []
    for n in range(C // LRU_BLOCK):
        blk = xcb[:, n * LRU_BLOCK:(n + 1) * LRU_BLOCK]
        ga.append(jnp.dot(blk, wa_ref[n], preferred_element_type=jnp.float32))
        gx.append(jnp.dot(blk, wx_ref[n], preferred_element_type=jnp.float32))
    ga = jnp.concatenate(ga, axis=1) if len(ga) > 1 else ga[0]
    gx = jnp.concatenate(gx, axis=1) if len(gx) > 1 else gx[0]
    r = _sigmoid(ga + ba_ref[...])
    i = _sigmoid(gx + bx_ref[...])
    nlam = -lam_ref[...]
    softplus = jnp.maximum(nlam, 0.0) + jnp.log1p(jnp.exp(-jnp.abs(nlam)))
    a = jnp.exp((-LRU_C * softplus) * r)
    b = jnp.sqrt(1.0 - a * a) * (i * xc)

    sub = lax.broadcasted_iota(jnp.int32, (T, C), 0) % SUBLANES
    for s in (1, 2, 4):
        if reverse:
            a_sh, b_sh = pltpu.roll(a, T - s, 0), pltpu.roll(b, T - s, 0)
            keep = sub < SUBLANES - s
        else:
            a_sh, b_sh = pltpu.roll(a, s, 0), pltpu.roll(b, s, 0)
            keep = sub >= s
        b = jnp.where(keep, a * b_sh + b, b)
        a = jnp.where(keep, a * a_sh, a)
    carry = carry_ref[...]
    n_tiles = T // SUBLANES
    tiles = [None] * n_tiles
    order = range(n_tiles - 1, -1, -1) if reverse else range(n_tiles)
    last = 0 if reverse else SUBLANES - 1
    for t in order:
        lo = t * SUBLANES
        ht = b[lo:lo + SUBLANES, :] + a[lo:lo + SUBLANES, :] * carry
        tiles[t] = ht
        carry = ht[last:last + 1, :]
    carry_ref[...] = carry
    return jnp.concatenate(tiles, axis=0)


def _lru_kernel(xf_ref, xfp_ref, xfn_ref, xb_ref, xbp_ref, xbn_ref, cw_ref, cb_ref,
                waf_ref, wxf_ref, baf_ref, bxf_ref, lamf_ref,
                wab_ref, wxb_ref, bab_ref, bxb_ref, lamb_ref,
                h_ref, carry_f, carry_b, *, n_tb):
    s = pl.program_id(1)
    T = xf_ref.shape[0]

    @pl.when(s == 0)
    def _():
        carry_f[...] = jnp.zeros_like(carry_f)
        carry_b[...] = jnp.zeros_like(carry_b)
        h_ref[...] = jnp.zeros_like(h_ref)

    hf = _lru_direction(xf_ref, xfp_ref, xfn_ref, cw_ref, cb_ref, waf_ref, wxf_ref, baf_ref, bxf_ref,
                        lamf_ref, carry_f, tb=s, n_tb=n_tb, reverse=False)
    rows_f = pl.ds(pl.multiple_of(s * T, T), T)
    h_ref[rows_f, :] += hf
    sb = n_tb - 1 - s
    hb = _lru_direction(xb_ref, xbp_ref, xbn_ref, cw_ref, cb_ref, wab_ref, wxb_ref, bab_ref, bxb_ref,
                        lamb_ref, carry_b, tb=sb, n_tb=n_tb, reverse=True)
    rows_b = pl.ds(pl.multiple_of(sb * T, T), T)
    h_ref[rows_b, :] += hb


def _rg_lru(xr, conv_w, conv_b, fwd, bwd, *, T=256, tc=256):
    S, C = xr.shape
    T, tc = min(T, S), min(tc, C)
    n_tb = S // T
    tpb = T // SUBLANES
    n_halo = S // SUBLANES
    nb = tc // LRU_BLOCK

    def main(tb_of):
        return pl.BlockSpec((T, tc), lambda c, s: (tb_of(s), c))

    def prev(tb_of):
        return pl.BlockSpec((SUBLANES, tc), lambda c, s: (jnp.maximum(tb_of(s) * tpb - 1, 0), c))

    def nxt(tb_of):
        return pl.BlockSpec((SUBLANES, tc), lambda c, s: (jnp.minimum((tb_of(s) + 1) * tpb, n_halo - 1), c))

    fwd_tb = lambda s: s
    bwd_tb = lambda s: n_tb - 1 - s
    row = pl.BlockSpec((1, tc), lambda c, s: (0, c))
    wspec = pl.BlockSpec((nb, LRU_BLOCK, LRU_BLOCK), lambda c, s: (c, 0, 0))
    return pl.pallas_call(
        functools.partial(_lru_kernel, n_tb=n_tb),
        grid=(C // tc, n_tb),
        in_specs=[main(fwd_tb), prev(fwd_tb), nxt(fwd_tb), main(bwd_tb), prev(bwd_tb), nxt(bwd_tb),
                  pl.BlockSpec((4, tc), lambda c, s: (0, c)), row,
                  wspec, wspec, row, row, row,
                  wspec, wspec, row, row, row],
        out_specs=pl.BlockSpec((S, tc), lambda c, s: (0, c)),
        out_shape=jax.ShapeDtypeStruct((S, C), jnp.float32),
        scratch_shapes=[pltpu.VMEM((1, tc), jnp.float32), pltpu.VMEM((1, tc), jnp.float32)],
        compiler_params=_cparams(("parallel", "arbitrary")), name="rg_lru",
    )(xr, xr, xr, xr, xr, xr, conv_w, conv_b, *fwd, *bwd)


def kernel(x, w_in, gate_b, conv_w, conv_b, lru_wa_fwd, lru_ba_fwd, lru_wx_fwd, lru_bx_fwd, lru_lam_fwd, lru_wa_bwd, lru_ba_bwd, lru_wx_bwd, lru_bx_bwd, lru_lam_bwd, p_attn, p_lru, w_out, ln1_g, ln1_b, w_mlp1, b_mlp1, w_mlp2, b_mlp2, ln2_g, ln2_b):
    B, S, D = x.shape
    depth = w_in.shape[0]
    lru_width = conv_w.shape[-1]
    alpha = (2.0 * depth) ** 0.25
    bf16, f32 = jnp.bfloat16, jnp.float32
    slopes = 2.0 ** (-8.0 * jnp.arange(1, N_HEADS + 1, dtype=f32) / N_HEADS)
    row = lambda v: v.reshape(1, -1).astype(f32)

    outs = []
    for bi in range(B):
        xs = x[bi]
        for l in range(depth):
            xb = xs.astype(bf16)
            w_in_b = w_in[l].astype(bf16)
            qkv, = _matmul(xb, w_in_b, name="proj_qkv", n_cols=3 * ATTN_WIDTH, col_off=0, epilogue=_epi_plain,
                           out_dtypes=[bf16])
            xr, = _matmul(xb, w_in_b, name="proj_lru", n_cols=lru_width, col_off=3 * ATTN_WIDTH,
                          epilogue=_epi_plain, out_dtypes=[f32])
            gates, = _matmul(xb, w_in_b, name="proj_gates", n_cols=2 * D, col_off=3 * ATTN_WIDTH + lru_width,
                             epilogue=_epi_gate, out_dtypes=[bf16], rows=[(row(gate_b[l]), 0)])

            pats = [_dilated_attention_pattern(qkv, slopes, d) for d in DILATIONS]
            attn = _combine([p[0] for p in pats], [p[1] for p in pats])

            lru = lambda wa, ba, wx, bx, lam: (wa[l].astype(bf16), wx[l].astype(bf16), row(ba[l]), row(bx[l]),
                                               row(lam[l]))
            h = _rg_lru(xr, conv_w[l], row(conv_b[l]),
                        lru(lru_wa_fwd, lru_ba_fwd, lru_wx_fwd, lru_bx_fwd, lru_lam_fwd),
                        lru(lru_wa_bwd, lru_ba_bwd, lru_wx_bwd, lru_bx_bwd, lru_lam_bwd))

            merged = _merge(attn, p_attn[l].astype(bf16), h, p_lru[l].astype(bf16), gates)
            u, = _matmul(merged, w_out[l].astype(bf16), name="out_proj", n_cols=D, col_off=0,
                         epilogue=functools.partial(_epi_residual, alpha=alpha), out_dtypes=[f32], tiles=[xs])
            y, yb = _layer_norm(u, row(ln1_g[l]), row(ln1_b[l]), [f32, bf16])

            hid, = _matmul(yb, w_mlp1[l].astype(bf16), name="mlp1", n_cols=w_mlp1.shape[-1], col_off=0,
                           epilogue=_epi_relu2, out_dtypes=[bf16], rows=[(row(b_mlp1[l]), 0)])
            u2, = _matmul(hid, w_mlp2[l].astype(bf16), name="mlp2", n_cols=D, col_off=0,
                          epilogue=functools.partial(_epi_residual_bias, alpha=alpha), out_dtypes=[f32],
                          rows=[(row(b_mlp2[l]), 0)], tiles=[y])
            xs, = _layer_norm(u2, row(ln2_g[l]), row(ln2_b[l]), [f32])
        outs.append(xs)
    return outs[0][None] if B == 1 else jnp.stack(outs, axis=0)
```

```python
import functools

import jax
import jax.numpy as jnp
from jax import lax
from jax.experimental import pallas as pl
from jax.experimental.pallas import tpu as pltpu

HEAD_DIM = 128
N_HEADS = 16
ATTN_WIDTH = N_HEADS * HEAD_DIM
DILATIONS = (1, 4, 16)
HALF = 64
LRU_BLOCK = 256
LRU_C = 8.0
LN_EPS = 1e-5
CONV_LEFT = 2

V7X_VMEM_LIMIT_BYTES = 56 * 1024 * 1024
SUBLANES = 8
LANES = 128

LRU_T = 512
LOG2_E = 1.4426950408889634
F32_TINY = 1e-37
QBLK = 128
KWIN = QBLK + 2 * HALF


def _cparams(sem):
    return pltpu.CompilerParams(dimension_semantics=sem, vmem_limit_bytes=V7X_VMEM_LIMIT_BYTES)


def _mm_kernel(*refs, nk, n_extra, n_out, epilogue):
    a_ref, b_ref = refs[0], refs[1]
    extra = refs[2:2 + n_extra]
    outs = refs[2 + n_extra:2 + n_extra + n_out]
    prod = jnp.dot(a_ref[...], b_ref[...], preferred_element_type=jnp.float32)
    if nk == 1:
        epilogue(prod, extra, outs)
        return
    acc_ref = refs[-1]
    k = pl.program_id(2)

    @pl.when(k == 0)
    def _():
        acc_ref[...] = jnp.zeros_like(acc_ref)

    acc_ref[...] += prod

    @pl.when(k == nk - 1)
    def _():
        epilogue(acc_ref[...], extra, outs)


def _matmul(a, b, *, name, n_cols, col_off, epilogue, out_dtypes, rows=(), tiles=(), tm=1024, tn=1024, tk=4096):
    M, K = a.shape
    tm, tn, tk = min(tm, M), min(tn, n_cols), min(tk, K)
    while col_off % tn or any(r_off % tn for _, r_off in rows):
        tn //= 2
    assert M % tm == 0 and n_cols % tn == 0 and K % tk == 0 and tn % LANES == 0
    nk = K // tk
    off = col_off // tn
    in_specs = [pl.BlockSpec((tm, tk), lambda i, j, k: (i, k)),
                pl.BlockSpec((tk, tn), lambda i, j, k: (k, j + off))]
    for r, r_off in rows:
        ro = r_off // tn
        in_specs.append(pl.BlockSpec((1, tn), lambda i, j, k, ro=ro: (0, j + ro)))
    for _ in tiles:
        in_specs.append(pl.BlockSpec((tm, tn), lambda i, j, k: (i, j)))
    out_specs = [pl.BlockSpec((tm, tn), lambda i, j, k: (i, j)) for _ in out_dtypes]
    out_shape = [jax.ShapeDtypeStruct((M, n_cols), dt) for dt in out_dtypes]
    n_extra = len(rows) + len(tiles)
    scratch = [pltpu.VMEM((tm, tn), jnp.float32)] if nk > 1 else []
    res = pl.pallas_call(
        functools.partial(_mm_kernel, nk=nk, n_extra=n_extra, n_out=len(out_dtypes), epilogue=epilogue),
        grid=(M // tm, n_cols // tn, nk),
        in_specs=in_specs, out_specs=out_specs, out_shape=out_shape, scratch_shapes=scratch,
        compiler_params=_cparams(("parallel", "parallel", "arbitrary")), name=name,
    )(a, b, *[r for r, _ in rows], *tiles)
    return res


def _mm_bias_kernel(a_ref, b_ref, bias_ref, o_ref):
    @pl.when(pl.program_id(2) == 0)
    def _():
        o_ref[...] = jnp.broadcast_to(bias_ref[...], o_ref.shape)

    o_ref[...] += jnp.dot(a_ref[...], b_ref[...], preferred_element_type=jnp.float32)


def _matmul_bias(a, b, bias, *, name, tm, tn, tk):
    M, K = a.shape
    N = b.shape[1]
    tm, tn, tk = min(tm, M), min(tn, N), min(tk, K)
    assert M % tm == 0 and N % tn == 0 and K % tk == 0
    return pl.pallas_call(
        _mm_bias_kernel,
        grid=(M // tm, N // tn, K // tk),
        in_specs=[pl.BlockSpec((tm, tk), lambda i, j, k: (i, k)),
                  pl.BlockSpec((tk, tn), lambda i, j, k: (k, j)),
                  pl.BlockSpec((1, tn), lambda i, j, k: (0, j))],
        out_specs=pl.BlockSpec((tm, tn), lambda i, j, k: (i, j)),
        out_shape=jax.ShapeDtypeStruct((M, N), jnp.float32),
        compiler_params=_cparams(("parallel", "parallel", "arbitrary")), name=name,
    )(a, b, bias)


def _epi_plain(acc, extra, outs):
    outs[0][...] = acc.astype(outs[0].dtype)


def _sigmoid(x):
    return 0.5 * jnp.tanh(0.5 * x) + 0.5


def _epi_gate(acc, extra, outs):
    outs[0][...] = _sigmoid(acc + extra[0][...]).astype(outs[0].dtype)


def _epi_relu2(acc, extra, outs):
    h = jnp.maximum(acc + extra[0][...], 0.0)
    outs[0][...] = (h * h).astype(outs[0].dtype)


def _epi_residual(acc, extra, outs, *, alpha):
    outs[0][...] = alpha * extra[0][...] + acc


def _merge_kernel(at_ref, pa_ref, h_ref, pl_ref, ga_ref, gb_ref, o_ref):
    pa = jnp.dot(at_ref[...], pa_ref[...], preferred_element_type=jnp.float32)
    out = ga_ref[...].astype(jnp.float32) * pa
    pb = jnp.dot(h_ref[...], pl_ref[...], preferred_element_type=jnp.float32)
    o_ref[...] = (out + gb_ref[...].astype(jnp.float32) * pb).astype(o_ref.dtype)


def _merge(attn, p_attn, h, p_lru, gates, *, tm=1024, tn=512):
    M, K1 = attn.shape
    K2 = h.shape[1]
    N = p_attn.shape[1]
    tm, tn = min(tm, M), min(tn, N)
    gb_off = N // tn
    return pl.pallas_call(
        _merge_kernel,
        grid=(M // tm, N // tn),
        in_specs=[
            pl.BlockSpec((tm, K1), lambda i, j: (i, 0)),
            pl.BlockSpec((K1, tn), lambda i, j: (0, j)),
            pl.BlockSpec((tm, K2), lambda i, j: (i, 0)),
            pl.BlockSpec((K2, tn), lambda i, j: (0, j)),
            pl.BlockSpec((tm, tn), lambda i, j: (i, j)),
            pl.BlockSpec((tm, tn), lambda i, j: (i, j + gb_off)),
        ],
        out_specs=pl.BlockSpec((tm, tn), lambda i, j: (i, j)),
        out_shape=jax.ShapeDtypeStruct((M, N), jnp.bfloat16),
        compiler_params=_cparams(("parallel", "arbitrary")), name="merge",
    )(attn, p_attn, h, p_lru, gates, gates)


def _ln_kernel(*refs, n_in, alpha):
    u = refs[0][...]
    if n_in == 2:
        u = alpha * refs[1][...] + u
    g_ref, b_ref = refs[n_in], refs[n_in + 1]
    mu = jnp.mean(u, axis=-1, keepdims=True)
    c = u - mu
    var = jnp.mean(c * c, axis=-1, keepdims=True)
    y = c * lax.rsqrt(var + LN_EPS) * g_ref[...] + b_ref[...]
    for o in refs[n_in + 2:]:
        o[...] = y.astype(o.dtype)


def _layer_norm(u, g, b, out_dtypes, *, res=None, alpha=1.0, tr=256):
    M, D = u.shape
    tr = min(tr, M)
    ins = [u] if res is None else [u, res]
    return pl.pallas_call(
        functools.partial(_ln_kernel, n_in=len(ins), alpha=alpha),
        grid=(M // tr,),
        in_specs=[pl.BlockSpec((tr, D), lambda i: (i, 0)) for _ in ins]
        + [pl.BlockSpec((1, D), lambda i: (0, 0)), pl.BlockSpec((1, D), lambda i: (0, 0))],
        out_specs=[pl.BlockSpec((tr, D), lambda i: (i, 0)) for _ in out_dtypes],
        out_shape=[jax.ShapeDtypeStruct((M, D), dt) for dt in out_dtypes],
        compiler_params=_cparams(("parallel",)), name="layer_norm",
    )(*ins, g, b)


def _attn_kernel(slope_ref, q_ref, k_ref, v_ref, o_ref, lse_ref, *, seq, tqs, scale):
    h = pl.program_id(1)
    step = pl.program_id(2)
    neg_slope = -slope_ref[h]
    n_sub = tqs // QBLK
    col = lax.broadcasted_iota(jnp.int32, (QBLK, KWIN), 1)
    row = lax.broadcasted_iota(jnp.int32, (QBLK, KWIN), 0)
    delta = col - row
    eye = (lax.broadcasted_iota(jnp.int32, (QBLK, QBLK), 0)
           == lax.broadcasted_iota(jnp.int32, (QBLK, QBLK), 1))

    def bias_for(off):
        dist = jnp.abs(delta + off)
        return jnp.where(dist <= HALF, dist.astype(jnp.float32) * neg_slope, -jnp.inf)

    interior_bias = bias_for(-HALF)
    for jj in range(n_sub):
        q0 = step * tqs + jj * QBLK
        if jj == 0 or jj == n_sub - 1:
            kstart = jnp.clip(q0 - HALF, 0, seq - KWIN)
            bias = bias_for(kstart - q0)
        else:
            kstart = q0 - HALF
            bias = interior_bias
        kstart = pl.multiple_of(kstart, HALF)
        q = q_ref[pl.ds(jj * QBLK, QBLK), :]
        kw = k_ref[pl.ds(kstart, KWIN), :]
        vw = v_ref[pl.ds(kstart, KWIN), :]
        s = lax.dot_general(q, kw, (((1,), (1,)), ((), ())), preferred_element_type=jnp.float32)
        s = s * scale + bias
        m = jnp.max(s, axis=-1, keepdims=True)
        e = jnp.exp(s - m)
        den = jnp.sum(e, axis=-1, keepdims=True)
        o = jnp.dot(e.astype(jnp.bfloat16), vw, preferred_element_type=jnp.float32) / den
        o_ref[pl.ds(jj * QBLK, QBLK), :] = o.astype(o_ref.dtype)
        lse = m + jnp.log(den)
        lse_row = jnp.sum(jnp.where(eye, lse, 0.0), axis=0, keepdims=True)
        lse_ref[0, :, pl.ds(jj * QBLK, QBLK)] = lse_row


def _dilated_attention_pattern(qkv, slopes, dil):
    S = qkv.shape[0]
    L = S // dil
    tqs = min(L, 1024)
    blocks_per_res = 3 * N_HEADS
    view = qkv.reshape(L, dil * 3 * ATTN_WIDTH)
    o, lse = pl.pallas_call(
        functools.partial(_attn_kernel, seq=L, tqs=tqs, scale=HEAD_DIM ** -0.5),
        grid=(dil, N_HEADS, L // tqs),
        in_specs=[
            pl.BlockSpec(memory_space=pltpu.SMEM),
            pl.BlockSpec((tqs, HEAD_DIM), lambda r, h, i: (i, r * blocks_per_res + h)),
            pl.BlockSpec((L, HEAD_DIM), lambda r, h, i: (0, r * blocks_per_res + N_HEADS + h)),
            pl.BlockSpec((L, HEAD_DIM), lambda r, h, i: (0, r * blocks_per_res + 2 * N_HEADS + h)),
        ],
        out_specs=[
            pl.BlockSpec((tqs, HEAD_DIM), lambda r, h, i: (i, r * N_HEADS + h)),
            pl.BlockSpec((1, 1, tqs), lambda r, h, i: (r * N_HEADS + h, 0, i)),
        ],
        out_shape=[jax.ShapeDtypeStruct((L, dil * ATTN_WIDTH), jnp.float32),
                   jax.ShapeDtypeStruct((dil * N_HEADS, 1, L), jnp.float32)],
        compiler_params=_cparams(("parallel", "parallel", "arbitrary")), name=f"attn_dil{dil}",
    )(slopes * float(dil), view, view, view)
    o = o.reshape(S, ATTN_WIDTH)
    lse = lse.reshape(dil, N_HEADS, L).transpose(2, 0, 1).reshape(S, N_HEADS)
    return o, lse


def _combine_kernel(o1_ref, o2_ref, o3_ref, l1_ref, l2_ref, l3_ref, out_ref):
    l1, l2, l3 = l1_ref[...], l2_ref[...], l3_ref[...]
    m = jnp.maximum(jnp.maximum(l1, l2), l3)
    w1, w2, w3 = jnp.exp(l1 - m), jnp.exp(l2 - m), jnp.exp(l3 - m)
    z = w1 + w2 + w3
    expand = (lax.broadcasted_iota(jnp.int32, (N_HEADS, ATTN_WIDTH), 1) // HEAD_DIM
              == lax.broadcasted_iota(jnp.int32, (N_HEADS, ATTN_WIDTH), 0)).astype(jnp.float32)

    def spread(w):
        return jnp.dot(w / z, expand, preferred_element_type=jnp.float32, precision=lax.Precision.HIGHEST)

    out = spread(w1) * o1_ref[...] + spread(w2) * o2_ref[...] + spread(w3) * o3_ref[...]
    out_ref[...] = out.astype(out_ref.dtype)


def _combine(os_, lses, *, tr=256):
    S = os_[0].shape[0]
    tr = min(tr, S)
    return pl.pallas_call(
        _combine_kernel,
        grid=(S // tr,),
        in_specs=[pl.BlockSpec((tr, ATTN_WIDTH), lambda i: (i, 0))] * 3
        + [pl.BlockSpec((tr, N_HEADS), lambda i: (i, 0))] * 3,
        out_specs=pl.BlockSpec((tr, ATTN_WIDTH), lambda i: (i, 0)),
        out_shape=jax.ShapeDtypeStruct((S, ATTN_WIDTH), jnp.bfloat16),
        compiler_params=_cparams(("parallel",)), name="attn_combine",
    )(*os_, *lses)


def _scan_local(a_slices, b_slices):
    hs, ps = [], []
    h = p = None
    for a, b in zip(a_slices, b_slices):
        h, p = (b, a) if h is None else (a * h + b, a * p)
        hs.append(h)
        ps.append(p)
    return hs, ps


def _slabs(ref, rows=None):
    parts = [ref[s] if rows is None else ref[s, rows, :] for s in range(ref.shape[0])]
    return jnp.concatenate(parts, axis=1) if len(parts) > 1 else parts[0]


def _to_slabs(ref, value, rows=None):
    for s in range(ref.shape[0]):
        piece = value[:, s * LANES:(s + 1) * LANES]
        if rows is None:
            ref[s] = piece
        else:
            ref[s, rows, :] = piece


def _lru_direction(x_ref, xp_ref, xn_ref, cw_ref, cb_ref, wa_ref, wx_ref, ba_ref, bx_ref, lam_ref,
                   carry_ref, ext_ref, t1a_ref, t1b_ref, e1_ref, g_ref, hs_ref, *, tb, n_tb, reverse):
    T, C = x_ref.shape
    n1 = T // SUBLANES
    n2 = n1 // SUBLANES
    strided = lambda k, n: pl.ds(k, n, stride=SUBLANES)
    _to_slabs(ext_ref, jnp.where(tb > 0, xp_ref[...], 0.0), pl.ds(0, SUBLANES))
    _to_slabs(ext_ref, x_ref[...], pl.ds(SUBLANES, T))
    _to_slabs(ext_ref, jnp.where(tb < n_tb - 1, xn_ref[...], 0.0), pl.ds(SUBLANES + T, SUBLANES))
    tap = {o: _slabs(ext_ref, strided(SUBLANES + o, n1)) for o in range(-CONV_LEFT, SUBLANES + 4 - CONV_LEFT - 1)}
    xk = []
    for k in range(SUBLANES):
        xck = cb_ref[...]
        for j in range(4):
            xck = xck + cw_ref[j:j + 1, :] * tap[k + j - CONV_LEFT]
        xk.append(xck)
    xk = jnp.concatenate(xk, axis=0)
    xkb = xk.astype(jnp.bfloat16)
    ga, gx = [], []
    for n in range(C // LRU_BLOCK):
        blk = xkb[:, n * LRU_BLOCK:(n + 1) * LRU_BLOCK]
        ga.append(jnp.dot(blk, wa_ref[n], preferred_element_type=jnp.float32))
        gx.append(jnp.dot(blk, wx_ref[n], preferred_element_type=jnp.float32))
    ga = jnp.concatenate(ga, axis=1) if len(ga) > 1 else ga[0]
    gx = jnp.concatenate(gx, axis=1) if len(gx) > 1 else gx[0]
    t_r = jnp.tanh(0.5 * ga + 0.5 * ba_ref[...])
    t_i = jnp.tanh(0.5 * gx + 0.5 * bx_ref[...])
    nlam = -lam_ref[...]
    softplus = jnp.maximum(nlam, 0.0) + jnp.log1p(jnp.exp(-jnp.abs(nlam)))
    half_rate = (-0.5 * LRU_C * LOG2_E) * softplus
    a = jnp.exp2(half_rate * t_r + half_rate)
    one_m_a2 = 1.0 - a * a
    root = one_m_a2 * lax.rsqrt(jnp.maximum(one_m_a2, F32_TINY))
    b = root * ((0.5 * t_i + 0.5) * xk)

    order = list(range(SUBLANES))[::-1] if reverse else list(range(SUBLANES))
    hs0, ps0 = _scan_local([a[k * n1:(k + 1) * n1] for k in order], [b[k * n1:(k + 1) * n1] for k in order])
    _to_slabs(t1a_ref, ps0[-1])
    _to_slabs(t1b_ref, hs0[-1])
    hs1, ps1 = _scan_local([_slabs(t1a_ref, strided(k, n2)) for k in order],
                           [_slabs(t1b_ref, strided(k, n2)) for k in order])
    grp_a, grp_b = ps1[-1], hs1[-1]
    carry = carry_ref[...]
    for g in (range(n2 - 1, -1, -1) if reverse else range(n2)):
        g_ref[g:g + 1, :] = carry
        carry = grp_a[g:g + 1, :] * carry + grp_b[g:g + 1, :]
    carry_ref[...] = carry
    enter_grp = g_ref[...]
    for j, k in enumerate(order):
        enter = enter_grp if j == 0 else hs1[j - 1] + ps1[j - 1] * enter_grp
        _to_slabs(e1_ref, enter, strided(k, n2))
    enter_tile = _slabs(e1_ref)
    for j, k in enumerate(order):
        _to_slabs(hs_ref, hs0[j] + ps0[j] * enter_tile, strided(k, n1))
    return _slabs(hs_ref)


def _lru_kernel(xf_ref, xfp_ref, xfn_ref, xb_ref, xbp_ref, xbn_ref, cw_ref, cb_ref,
                waf_ref, wxf_ref, baf_ref, bxf_ref, lamf_ref,
                wab_ref, wxb_ref, bab_ref, bxb_ref, lamb_ref,
                h_ref, acc_ref, carry_f, carry_b, *scan_scratch, n_tb):
    s = pl.program_id(1)
    T = xf_ref.shape[0]

    @pl.when(s == 0)
    def _():
        carry_f[...] = jnp.zeros_like(carry_f)
        carry_b[...] = jnp.zeros_like(carry_b)
        acc_ref[...] = jnp.zeros_like(acc_ref)

    def emit(rows, h):
        total = acc_ref[rows, :] + h
        acc_ref[rows, :] = total
        h_ref[rows, :] = total.astype(h_ref.dtype)

    hf = _lru_direction(xf_ref, xfp_ref, xfn_ref, cw_ref, cb_ref, waf_ref, wxf_ref, baf_ref, bxf_ref,
                        lamf_ref, carry_f, *scan_scratch, tb=s, n_tb=n_tb, reverse=False)
    emit(pl.ds(pl.multiple_of(s * T, T), T), hf)
    sb = n_tb - 1 - s
    hb = _lru_direction(xb_ref, xbp_ref, xbn_ref, cw_ref, cb_ref, wab_ref, wxb_ref, bab_ref, bxb_ref,
                        lamb_ref, carry_b, *scan_scratch, tb=sb, n_tb=n_tb, reverse=True)
    emit(pl.ds(pl.multiple_of(sb * T, T), T), hb)


def _rg_lru(xr, conv_w, conv_b, fwd, bwd, *, tc=512):
    S, C = xr.shape
    T, tc = LRU_T, min(tc, C)
    assert S % T == 0 and C % tc == 0 and tc % LRU_BLOCK == 0
    n_tb = S // T
    tpb = T // SUBLANES
    n_halo = S // SUBLANES
    nb = tc // LRU_BLOCK
    n_slab = tc // LANES
    n1 = T // SUBLANES

    def main(tb_of):
        return pl.BlockSpec((T, tc), lambda c, s: (tb_of(s), c))

    def prev(tb_of):
        return pl.BlockSpec((SUBLANES, tc), lambda c, s: (jnp.maximum(tb_of(s) * tpb - 1, 0), c))

    def nxt(tb_of):
        return pl.BlockSpec((SUBLANES, tc), lambda c, s: (jnp.minimum((tb_of(s) + 1) * tpb, n_halo - 1), c))

    fwd_tb = lambda s: s
    bwd_tb = lambda s: n_tb - 1 - s
    row = pl.BlockSpec((1, tc), lambda c, s: (0, c))
    wspec = pl.BlockSpec((nb, LRU_BLOCK, LRU_BLOCK), lambda c, s: (c, 0, 0))
    return pl.pallas_call(
        functools.partial(_lru_kernel, n_tb=n_tb),
        grid=(C // tc, n_tb),
        in_specs=[main(fwd_tb), prev(fwd_tb), nxt(fwd_tb), main(bwd_tb), prev(bwd_tb), nxt(bwd_tb),
                  pl.BlockSpec((4, tc), lambda c, s: (0, c)), row,
                  wspec, wspec, row, row, row,
                  wspec, wspec, row, row, row],
        out_specs=pl.BlockSpec((S, tc), lambda c, s: (0, c)),
        out_shape=jax.ShapeDtypeStruct((S, C), jnp.bfloat16),
        scratch_shapes=[pltpu.VMEM((S, tc), jnp.float32),
                        pltpu.VMEM((1, tc), jnp.float32), pltpu.VMEM((1, tc), jnp.float32),
                        pltpu.VMEM((n_slab, T + 2 * SUBLANES, LANES), jnp.float32),
                        pltpu.VMEM((n_slab, n1, LANES), jnp.float32),
                        pltpu.VMEM((n_slab, n1, LANES), jnp.float32),
                        pltpu.VMEM((n_slab, n1, LANES), jnp.float32),
                        pltpu.VMEM((n1 // SUBLANES, tc), jnp.float32),
                        pltpu.VMEM((n_slab, T, LANES), jnp.float32)],
        compiler_params=_cparams(("parallel", "arbitrary")), name="rg_lru",
    )(xr, xr, xr, xr, xr, xr, conv_w, conv_b, *fwd, *bwd)


def kernel(x, w_in, gate_b, conv_w, conv_b, lru_wa_fwd, lru_ba_fwd, lru_wx_fwd, lru_bx_fwd, lru_lam_fwd, lru_wa_bwd, lru_ba_bwd, lru_wx_bwd, lru_bx_bwd, lru_lam_bwd, p_attn, p_lru, w_out, ln1_g, ln1_b, w_mlp1, b_mlp1, w_mlp2, b_mlp2, ln2_g, ln2_b):
    B, S, D = x.shape
    depth = w_in.shape[0]
    lru_width = conv_w.shape[-1]
    alpha = (2.0 * depth) ** 0.25
    bf16, f32 = jnp.bfloat16, jnp.float32
    slopes = 2.0 ** (-8.0 * jnp.arange(1, N_HEADS + 1, dtype=f32) / N_HEADS)
    row = lambda v: v.reshape(1, -1).astype(f32)

    outs = []
    for bi in range(B):
        xs = x[bi]
        for l in range(depth):
            xb = xs.astype(bf16)
            w_in_b = w_in[l].astype(bf16)
            qkv, = _matmul(xb, w_in_b, name="proj_qkv", n_cols=3 * ATTN_WIDTH, col_off=0, epilogue=_epi_plain,
                           out_dtypes=[bf16])
            xr, = _matmul(xb, w_in_b, name="proj_lru", n_cols=lru_width, col_off=3 * ATTN_WIDTH,
                          epilogue=_epi_plain, out_dtypes=[f32])
            gates, = _matmul(xb, w_in_b, name="proj_gates", n_cols=2 * D, col_off=3 * ATTN_WIDTH + lru_width,
                             epilogue=_epi_gate, out_dtypes=[bf16], rows=[(row(gate_b[l]), 0)])

            pats = [_dilated_attention_pattern(qkv, slopes, d) for d in DILATIONS]
            attn = _combine([p[0] for p in pats], [p[1] for p in pats])

            lru = lambda wa, ba, wx, bx, lam: (wa[l].astype(bf16), wx[l].astype(bf16), row(ba[l]), row(bx[l]),
                                               row(lam[l]))
            h = _rg_lru(xr, conv_w[l], row(conv_b[l]),
                        lru(lru_wa_fwd, lru_ba_fwd, lru_wx_fwd, lru_bx_fwd, lru_lam_fwd),
                        lru(lru_wa_bwd, lru_ba_bwd, lru_wx_bwd, lru_bx_bwd, lru_lam_bwd))

            merged = _merge(attn, p_attn[l].astype(bf16), h, p_lru[l].astype(bf16), gates)
            u, = _matmul(merged, w_out[l].astype(bf16), name="out_proj", n_cols=D, col_off=0,
                         epilogue=functools.partial(_epi_residual, alpha=alpha), out_dtypes=[f32], tiles=[xs])
            y, yb = _layer_norm(u, row(ln1_g[l]), row(ln1_b[l]), [f32, bf16])

            hid, = _matmul(yb, w_mlp1[l].astype(bf16), name="mlp1", n_cols=w_mlp1.shape[-1], col_off=0,
                           epilogue=_epi_relu2, out_dtypes=[bf16], rows=[(row(b_mlp1[l]), 0)])
            u2 = _matmul_bias(hid, w_mlp2[l].astype(bf16), row(b_mlp2[l]), name="mlp2", tm=1024, tn=2048, tk=2048)
            xs, = _layer_norm(u2, row(ln2_g[l]), row(ln2_b[l]), [f32], res=y, alpha=alpha)
        outs.append(xs)
    return outs[0][None] if B == 1 else jnp.stack(outs, axis=0)
```

```python
import functools

import jax
import jax.numpy as jnp
from jax import lax
from jax.experimental import pallas as pl
from jax.experimental.pallas import tpu as pltpu

HEAD_DIM = 128
N_HEADS = 16
ATTN_WIDTH = N_HEADS * HEAD_DIM
DILATIONS = (1, 4, 16)
HALF = 64
LRU_BLOCK = 256
LRU_C = 8.0
LN_EPS = 1e-5
CONV_LEFT = 2

V7X_VMEM_LIMIT_BYTES = 56 * 1024 * 1024
SUBLANES = 8
LANES = 128

LRU_T = 512
LOG2_E = 1.4426950408889634
F32_TINY = 1e-37
QBLK = 128
KWIN = QBLK + 2 * HALF
ATT_CHUNK = 8
RESIDUE_SLOTS = 2
F32_WEIGHT_TN = 512


def _cparams(sem):
    return pltpu.CompilerParams(dimension_semantics=sem, vmem_limit_bytes=V7X_VMEM_LIMIT_BYTES)


def _mm_kernel(*refs, nk, n_extra, n_out, epilogue):
    a_ref, b_ref = refs[0], refs[1]
    extra = refs[2:2 + n_extra]
    outs = refs[2 + n_extra:2 + n_extra + n_out]
    prod = jnp.dot(a_ref[...], b_ref[...].astype(jnp.bfloat16), preferred_element_type=jnp.float32)
    if nk == 1:
        epilogue(prod, extra, outs)
        return
    acc_ref = refs[-1]
    k = pl.program_id(2)

    @pl.when(k == 0)
    def _():
        acc_ref[...] = jnp.zeros_like(acc_ref)

    acc_ref[...] += prod

    @pl.when(k == nk - 1)
    def _():
        epilogue(acc_ref[...], extra, outs)


def _matmul(a, b, *, name, n_cols, col_off, epilogue, out_dtypes, rows=(), tiles=(), tm=1024, tn=1024, tk=4096):
    M, K = a.shape
    tm, tn, tk = min(tm, M), min(tn, n_cols), min(tk, K)
    while col_off % tn or any(r_off % tn for _, r_off in rows):
        tn //= 2
    assert M % tm == 0 and n_cols % tn == 0 and K % tk == 0 and tn % LANES == 0
    nk = K // tk
    off = col_off // tn
    in_specs = [pl.BlockSpec((tm, tk), lambda i, j, k: (i, k)),
                pl.BlockSpec((tk, tn), lambda i, j, k: (k, j + off))]
    for r, r_off in rows:
        ro = r_off // tn
        in_specs.append(pl.BlockSpec((1, tn), lambda i, j, k, ro=ro: (0, j + ro)))
    for _ in tiles:
        in_specs.append(pl.BlockSpec((tm, tn), lambda i, j, k: (i, j)))
    out_specs = [pl.BlockSpec((tm, tn), lambda i, j, k: (i, j)) for _ in out_dtypes]
    out_shape = [jax.ShapeDtypeStruct((M, n_cols), dt) for dt in out_dtypes]
    n_extra = len(rows) + len(tiles)
    scratch = [pltpu.VMEM((tm, tn), jnp.float32)] if nk > 1 else []
    res = pl.pallas_call(
        functools.partial(_mm_kernel, nk=nk, n_extra=n_extra, n_out=len(out_dtypes), epilogue=epilogue),
        grid=(M // tm, n_cols // tn, nk),
        in_specs=in_specs, out_specs=out_specs, out_shape=out_shape, scratch_shapes=scratch,
        compiler_params=_cparams(("parallel", "parallel", "arbitrary")), name=name,
    )(a, b, *[r for r, _ in rows], *tiles)
    return res


def _mm_bias_kernel(a_ref, b_ref, bias_ref, o_ref):
    @pl.when(pl.program_id(2) == 0)
    def _():
        o_ref[...] = jnp.broadcast_to(bias_ref[...], o_ref.shape)

    o_ref[...] += jnp.dot(a_ref[...], b_ref[...], preferred_element_type=jnp.float32)


def _matmul_bias(a, b, bias, *, name, tm, tn, tk):
    M, K = a.shape
    N = b.shape[1]
    tm, tn, tk = min(tm, M), min(tn, N), min(tk, K)
    assert M % tm == 0 and N % tn == 0 and K % tk == 0
    return pl.pallas_call(
        _mm_bias_kernel,
        grid=(M // tm, N // tn, K // tk),
        in_specs=[pl.BlockSpec((tm, tk), lambda i, j, k: (i, k)),
                  pl.BlockSpec((tk, tn), lambda i, j, k: (k, j)),
                  pl.BlockSpec((1, tn), lambda i, j, k: (0, j))],
        out_specs=pl.BlockSpec((tm, tn), lambda i, j, k: (i, j)),
        out_shape=jax.ShapeDtypeStruct((M, N), jnp.float32),
        compiler_params=_cparams(("parallel", "parallel", "arbitrary")), name=name,
    )(a, b, bias)


def _epi_plain(acc, extra, outs):
    outs[0][...] = acc.astype(outs[0].dtype)


def _sigmoid(x):
    return 0.5 * jnp.tanh(0.5 * x) + 0.5


def _epi_gate(acc, extra, outs):
    outs[0][...] = _sigmoid(acc + extra[0][...]).astype(outs[0].dtype)


def _epi_relu2(acc, extra, outs):
    h = jnp.maximum(acc + extra[0][...], 0.0)
    outs[0][...] = (h * h).astype(outs[0].dtype)


def _epi_residual(acc, extra, outs, *, alpha):
    outs[0][...] = alpha * extra[0][...] + acc


def _merge_kernel(at_ref, pa_ref, h_ref, pl_ref, ga_ref, gb_ref, o_ref):
    pa = jnp.dot(at_ref[...], pa_ref[...].astype(jnp.bfloat16), preferred_element_type=jnp.float32)
    out = ga_ref[...].astype(jnp.float32) * pa
    pb = jnp.dot(h_ref[...], pl_ref[...].astype(jnp.bfloat16), preferred_element_type=jnp.float32)
    o_ref[...] = (out + gb_ref[...].astype(jnp.float32) * pb).astype(o_ref.dtype)


def _merge(attn, p_attn, h, p_lru, gates, *, tm=1024, tn=512):
    M, K1 = attn.shape
    K2 = h.shape[1]
    N = p_attn.shape[1]
    tm, tn = min(tm, M), min(tn, N)
    gb_off = N // tn
    return pl.pallas_call(
        _merge_kernel,
        grid=(M // tm, N // tn),
        in_specs=[
            pl.BlockSpec((tm, K1), lambda i, j: (i, 0)),
            pl.BlockSpec((K1, tn), lambda i, j: (0, j)),
            pl.BlockSpec((tm, K2), lambda i, j: (i, 0)),
            pl.BlockSpec((K2, tn), lambda i, j: (0, j)),
            pl.BlockSpec((tm, tn), lambda i, j: (i, j)),
            pl.BlockSpec((tm, tn), lambda i, j: (i, j + gb_off)),
        ],
        out_specs=pl.BlockSpec((tm, tn), lambda i, j: (i, j)),
        out_shape=jax.ShapeDtypeStruct((M, N), jnp.bfloat16),
        compiler_params=_cparams(("parallel", "arbitrary")), name="merge",
    )(attn, p_attn, h, p_lru, gates, gates)


def _ln_kernel(*refs, n_in, alpha):
    u = refs[0][...]
    if n_in == 2:
        u = alpha * refs[1][...] + u
    g_ref, b_ref = refs[n_in], refs[n_in + 1]
    mu = jnp.mean(u, axis=-1, keepdims=True)
    c = u - mu
    var = jnp.mean(c * c, axis=-1, keepdims=True)
    y = c * lax.rsqrt(var + LN_EPS) * g_ref[...] + b_ref[...]
    for o in refs[n_in + 2:]:
        o[...] = y.astype(o.dtype)


def _layer_norm(u, g, b, out_dtypes, *, res=None, alpha=1.0, tr=256):
    M, D = u.shape
    tr = min(tr, M)
    ins = [u] if res is None else [u, res]
    return pl.pallas_call(
        functools.partial(_ln_kernel, n_in=len(ins), alpha=alpha),
        grid=(M // tr,),
        in_specs=[pl.BlockSpec((tr, D), lambda i: (i, 0)) for _ in ins]
        + [pl.BlockSpec((1, D), lambda i: (0, 0)), pl.BlockSpec((1, D), lambda i: (0, 0))],
        out_specs=[pl.BlockSpec((tr, D), lambda i: (i, 0)) for _ in out_dtypes],
        out_shape=[jax.ShapeDtypeStruct((M, D), dt) for dt in out_dtypes],
        compiler_params=_cparams(("parallel",)), name="layer_norm",
    )(*ins, g, b)


def _band_chunk(q_ref, k_ref, v_ref, bias_ref, *, seq, chunk, scale, emit):
    n_sub = min(ATT_CHUNK, seq // QBLK)
    for jj in range(n_sub):
        q0 = (chunk * n_sub + jj) * QBLK
        if jj in (0, n_sub - 1):
            kstart = jnp.clip(q0 - HALF, 0, seq - KWIN)
            bias = bias_ref[jnp.where(q0 == 0, 1, jnp.where(q0 == seq - QBLK, 2, 0))]
        else:
            kstart = q0 - HALF
            bias = bias_ref[0]
        kstart = pl.multiple_of(kstart, HALF)
        q = q_ref[pl.ds(pl.multiple_of(q0, QBLK), QBLK), :]
        kw = k_ref[pl.ds(kstart, KWIN), :]
        vw = v_ref[pl.ds(kstart, KWIN), :]
        s = lax.dot_general(q, kw, (((1,), (1,)), ((), ())), preferred_element_type=jnp.float32)
        s = s * scale + bias
        m = jnp.max(s, axis=-1, keepdims=True)
        e = jnp.exp(s - m)
        den = jnp.sum(e, axis=-1, keepdims=True)
        o = jnp.dot(e.astype(jnp.bfloat16), vw, preferred_element_type=jnp.float32) / den
        emit(q0, o, m + jnp.log(den))


def _attn_kernel(slope_ref, q_ref, k_ref, v_ref, out_ref,
                 qf_ref, kf_ref, vf_ref, qd_ref, kd_ref, vd_ref, o_mid_ref, l_mid_ref, o_far_ref, l_far_ref,
                 bias_ref, *, scale):
    S = q_ref.shape[0]
    slope = slope_ref[pl.program_id(0)]
    col = lax.broadcasted_iota(jnp.int32, (QBLK, KWIN), 1)
    row = lax.broadcasted_iota(jnp.int32, (QBLK, KWIN), 0)
    delta = col - row

    def set_bias(dil):
        for case, off in enumerate((-HALF, 0, -2 * HALF)):
            dist = jnp.abs(delta + off)
            bias_ref[case] = jnp.where(dist <= HALF, dist.astype(jnp.float32) * (-slope * dil), -jnp.inf)

    piece = min(S, 1024)

    def widen(i, _):
        rows = pl.ds(pl.multiple_of(i * piece, piece), piece)
        qf_ref[rows, :] = q_ref[rows, :].astype(jnp.float32)
        kf_ref[rows, :] = k_ref[rows, :].astype(jnp.float32)
        vf_ref[rows, :] = v_ref[rows, :].astype(jnp.float32)
        return 0

    lax.fori_loop(0, S // piece, widen, 0)

    for dil, o_ref, l_ref in ((DILATIONS[2], o_far_ref, l_far_ref), (DILATIONS[1], o_mid_ref, l_mid_ref)):
        L = S // dil
        set_bias(dil)

        n_chunks = L // (min(ATT_CHUNK, L // QBLK) * QBLK)
        group = min(dil, RESIDUE_SLOTS) if n_chunks == 1 else 1

        def residues(i, _, dil=dil, L=L, o_ref=o_ref, l_ref=l_ref, n_chunks=n_chunks, group=group):
            for slot in range(group):
                r = i * group + slot
                gather = pl.ds(r, L, stride=dil)
                qd, kd, vd = qd_ref.at[slot], kd_ref.at[slot], vd_ref.at[slot]
                qd[pl.ds(0, L), :] = qf_ref[gather, :].astype(jnp.bfloat16)
                kd[pl.ds(0, L), :] = kf_ref[gather, :].astype(jnp.bfloat16)
                vd[pl.ds(0, L), :] = vf_ref[gather, :].astype(jnp.bfloat16)

                def emit(q0, o, lse, r=r):
                    scatter = pl.ds(q0 * dil + r, QBLK, stride=dil)
                    o_ref[scatter, :] = o
                    l_ref[scatter, :] = jnp.broadcast_to(lse, (QBLK, HEAD_DIM))

                def chunk(c, _, qd=qd, kd=kd, vd=vd, emit=emit):
                    _band_chunk(qd, kd, vd, bias_ref, seq=L, chunk=c, scale=scale, emit=emit)
                    return 0

                lax.fori_loop(0, n_chunks, chunk, 0)
            return 0

        lax.fori_loop(0, dil // group, residues, 0)

    set_bias(DILATIONS[0])

    def emit_mixed(q0, o_near, l_near):
        rows = pl.ds(pl.multiple_of(q0, QBLK), QBLK)
        l_mid, l_far = l_mid_ref[rows, :], l_far_ref[rows, :]
        m = jnp.maximum(jnp.maximum(l_mid, l_far), l_near)
        w_near, w_mid, w_far = jnp.exp(l_near - m), jnp.exp(l_mid - m), jnp.exp(l_far - m)
        mixed = w_near * o_near + w_mid * o_mid_ref[rows, :] + w_far * o_far_ref[rows, :]
        out_ref[rows, :] = (mixed / (w_near + w_mid + w_far)).astype(out_ref.dtype)

    def chunk_near(c, _):
        _band_chunk(q_ref, k_ref, v_ref, bias_ref, seq=S, chunk=c, scale=scale, emit=emit_mixed)
        return 0

    lax.fori_loop(0, S // (ATT_CHUNK * QBLK), chunk_near, 0)


def _dilated_attention(qkv, slopes):
    S = qkv.shape[0]
    assert DILATIONS[0] == 1 and S % (DILATIONS[2] * KWIN) == 0 and S % (ATT_CHUNK * QBLK) == 0
    f32, bf16 = jnp.float32, jnp.bfloat16
    l_mid = S // DILATIONS[1]
    head_col = lambda part: pl.BlockSpec((S, HEAD_DIM), lambda h: (0, part * N_HEADS + h))
    return pl.pallas_call(
        functools.partial(_attn_kernel, scale=HEAD_DIM ** -0.5),
        grid=(N_HEADS,),
        in_specs=[pl.BlockSpec(memory_space=pltpu.SMEM), head_col(0), head_col(1), head_col(2)],
        out_specs=pl.BlockSpec((S, HEAD_DIM), lambda h: (0, h)),
        out_shape=jax.ShapeDtypeStruct((S, ATTN_WIDTH), bf16),
        scratch_shapes=[pltpu.VMEM((S, HEAD_DIM), f32)] * 3
        + [pltpu.VMEM((RESIDUE_SLOTS, l_mid, HEAD_DIM), bf16)] * 3
        + [pltpu.VMEM((S, HEAD_DIM), f32)] * 4
        + [pltpu.VMEM((3, QBLK, KWIN), f32)],
        compiler_params=_cparams(("parallel",)), name="dilated_attention",
    )(slopes, qkv, qkv, qkv)


def _scan_local(a_slices, b_slices):
    hs, ps = [], []
    h = p = None
    for a, b in zip(a_slices, b_slices):
        h, p = (b, a) if h is None else (a * h + b, a * p)
        hs.append(h)
        ps.append(p)
    return hs, ps


def _slabs(ref, rows=None):
    parts = [ref[s] if rows is None else ref[s, rows, :] for s in range(ref.shape[0])]
    return jnp.concatenate(parts, axis=1) if len(parts) > 1 else parts[0]


def _to_slabs(ref, value, rows=None):
    for s in range(ref.shape[0]):
        piece = value[:, s * LANES:(s + 1) * LANES]
        if rows is None:
            ref[s] = piece
        else:
            ref[s, rows, :] = piece


def _lru_direction(x_ref, xp_ref, xn_ref, cw_ref, cb_ref, wa_ref, wx_ref, ba_ref, bx_ref, lam_ref,
                   carry_ref, ext_ref, t1a_ref, t1b_ref, e1_ref, g_ref, hs_ref, *, tb, n_tb, reverse):
    T, C = x_ref.shape
    n1 = T // SUBLANES
    n2 = n1 // SUBLANES
    strided = lambda k, n: pl.ds(k, n, stride=SUBLANES)
    _to_slabs(ext_ref, jnp.where(tb > 0, xp_ref[...], 0.0), pl.ds(0, SUBLANES))
    _to_slabs(ext_ref, x_ref[...], pl.ds(SUBLANES, T))
    _to_slabs(ext_ref, jnp.where(tb < n_tb - 1, xn_ref[...], 0.0), pl.ds(SUBLANES + T, SUBLANES))
    tap = {o: _slabs(ext_ref, strided(SUBLANES + o, n1)) for o in range(-CONV_LEFT, SUBLANES + 4 - CONV_LEFT - 1)}
    xk = []
    for k in range(SUBLANES):
        xck = cb_ref[...]
        for j in range(4):
            xck = xck + cw_ref[j:j + 1, :] * tap[k + j - CONV_LEFT]
        xk.append(xck)
    xk = jnp.concatenate(xk, axis=0)
    xkb = xk.astype(jnp.bfloat16)
    ga, gx = [], []
    for n in range(C // LRU_BLOCK):
        blk = xkb[:, n * LRU_BLOCK:(n + 1) * LRU_BLOCK]
        ga.append(jnp.dot(blk, wa_ref[n], preferred_element_type=jnp.float32))
        gx.append(jnp.dot(blk, wx_ref[n], preferred_element_type=jnp.float32))
    ga = jnp.concatenate(ga, axis=1) if len(ga) > 1 else ga[0]
    gx = jnp.concatenate(gx, axis=1) if len(gx) > 1 else gx[0]
    t_r = jnp.tanh(0.5 * ga + 0.5 * ba_ref[...])
    t_i = jnp.tanh(0.5 * gx + 0.5 * bx_ref[...])
    nlam = -lam_ref[...]
    softplus = jnp.maximum(nlam, 0.0) + jnp.log1p(jnp.exp(-jnp.abs(nlam)))
    half_rate = (-0.5 * LRU_C * LOG2_E) * softplus
    a = jnp.exp2(half_rate * t_r + half_rate)
    one_m_a2 = 1.0 - a * a
    root = one_m_a2 * lax.rsqrt(jnp.maximum(one_m_a2, F32_TINY))
    b = root * ((0.5 * t_i + 0.5) * xk)

    order = list(range(SUBLANES))[::-1] if reverse else list(range(SUBLANES))
    hs0, ps0 = _scan_local([a[k * n1:(k + 1) * n1] for k in order], [b[k * n1:(k + 1) * n1] for k in order])
    _to_slabs(t1a_ref, ps0[-1])
    _to_slabs(t1b_ref, hs0[-1])
    hs1, ps1 = _scan_local([_slabs(t1a_ref, strided(k, n2)) for k in order],
                           [_slabs(t1b_ref, strided(k, n2)) for k in order])
    grp_a, grp_b = ps1[-1], hs1[-1]
    carry = carry_ref[...]
    for g in (range(n2 - 1, -1, -1) if reverse else range(n2)):
        g_ref[g:g + 1, :] = carry
        carry = grp_a[g:g + 1, :] * carry + grp_b[g:g + 1, :]
    carry_ref[...] = carry
    enter_grp = g_ref[...]
    for j, k in enumerate(order):
        enter = enter_grp if j == 0 else hs1[j - 1] + ps1[j - 1] * enter_grp
        _to_slabs(e1_ref, enter, strided(k, n2))
    enter_tile = _slabs(e1_ref)
    for j, k in enumerate(order):
        _to_slabs(hs_ref, hs0[j] + ps0[j] * enter_tile, strided(k, n1))
    return _slabs(hs_ref)


def _lru_kernel(xf_ref, xfp_ref, xfn_ref, xb_ref, xbp_ref, xbn_ref, cw_ref, cb_ref,
                waf_ref, wxf_ref, baf_ref, bxf_ref, lamf_ref,
                wab_ref, wxb_ref, bab_ref, bxb_ref, lamb_ref,
                h_ref, acc_ref, carry_f, carry_b, *scan_scratch, n_tb):
    s = pl.program_id(1)
    T = xf_ref.shape[0]

    @pl.when(s == 0)
    def _():
        carry_f[...] = jnp.zeros_like(carry_f)
        carry_b[...] = jnp.zeros_like(carry_b)
        acc_ref[...] = jnp.zeros_like(acc_ref)

    def emit(rows, h):
        total = acc_ref[rows, :] + h
        acc_ref[rows, :] = total
        h_ref[rows, :] = total.astype(h_ref.dtype)

    hf = _lru_direction(xf_ref, xfp_ref, xfn_ref, cw_ref, cb_ref, waf_ref, wxf_ref, baf_ref, bxf_ref,
                        lamf_ref, carry_f, *scan_scratch, tb=s, n_tb=n_tb, reverse=False)
    emit(pl.ds(pl.multiple_of(s * T, T), T), hf)
    sb = n_tb - 1 - s
    hb = _lru_direction(xb_ref, xbp_ref, xbn_ref, cw_ref, cb_ref, wab_ref, wxb_ref, bab_ref, bxb_ref,
                        lamb_ref, carry_b, *scan_scratch, tb=sb, n_tb=n_tb, reverse=True)
    emit(pl.ds(pl.multiple_of(sb * T, T), T), hb)


def _rg_lru(xr, conv_w, conv_b, fwd, bwd, *, tc=512):
    S, C = xr.shape
    T, tc = LRU_T, min(tc, C)
    assert S % T == 0 and C % tc == 0 and tc % LRU_BLOCK == 0
    n_tb = S // T
    tpb = T // SUBLANES
    n_halo = S // SUBLANES
    nb = tc // LRU_BLOCK
    n_slab = tc // LANES
    n1 = T // SUBLANES

    def main(tb_of):
        return pl.BlockSpec((T, tc), lambda c, s: (tb_of(s), c))

    def prev(tb_of):
        return pl.BlockSpec((SUBLANES, tc), lambda c, s: (jnp.maximum(tb_of(s) * tpb - 1, 0), c))

    def nxt(tb_of):
        return pl.BlockSpec((SUBLANES, tc), lambda c, s: (jnp.minimum((tb_of(s) + 1) * tpb, n_halo - 1), c))

    fwd_tb = lambda s: s
    bwd_tb = lambda s: n_tb - 1 - s
    row = pl.BlockSpec((1, tc), lambda c, s: (0, c))
    wspec = pl.BlockSpec((nb, LRU_BLOCK, LRU_BLOCK), lambda c, s: (c, 0, 0))
    return pl.pallas_call(
        functools.partial(_lru_kernel, n_tb=n_tb),
        grid=(C // tc, n_tb),
        in_specs=[main(fwd_tb), prev(fwd_tb), nxt(fwd_tb), main(bwd_tb), prev(bwd_tb), nxt(bwd_tb),
                  pl.BlockSpec((4, tc), lambda c, s: (0, c)), row,
                  wspec, wspec, row, row, row,
                  wspec, wspec, row, row, row],
        out_specs=pl.BlockSpec((S, tc), lambda c, s: (0, c)),
        out_shape=jax.ShapeDtypeStruct((S, C), jnp.bfloat16),
        scratch_shapes=[pltpu.VMEM((S, tc), jnp.float32),
                        pltpu.VMEM((1, tc), jnp.float32), pltpu.VMEM((1, tc), jnp.float32),
                        pltpu.VMEM((n_slab, T + 2 * SUBLANES, LANES), jnp.float32),
                        pltpu.VMEM((n_slab, n1, LANES), jnp.float32),
                        pltpu.VMEM((n_slab, n1, LANES), jnp.float32),
                        pltpu.VMEM((n_slab, n1, LANES), jnp.float32),
                        pltpu.VMEM((n1 // SUBLANES, tc), jnp.float32),
                        pltpu.VMEM((n_slab, T, LANES), jnp.float32)],
        compiler_params=_cparams(("parallel", "arbitrary")), name="rg_lru",
    )(xr, xr, xr, xr, xr, xr, conv_w, conv_b, *fwd, *bwd)


def kernel(x, w_in, gate_b, conv_w, conv_b, lru_wa_fwd, lru_ba_fwd, lru_wx_fwd, lru_bx_fwd, lru_lam_fwd, lru_wa_bwd, lru_ba_bwd, lru_wx_bwd, lru_bx_bwd, lru_lam_bwd, p_attn, p_lru, w_out, ln1_g, ln1_b, w_mlp1, b_mlp1, w_mlp2, b_mlp2, ln2_g, ln2_b):
    B, S, D = x.shape
    depth = w_in.shape[0]
    lru_width = conv_w.shape[-1]
    alpha = (2.0 * depth) ** 0.25
    bf16, f32 = jnp.bfloat16, jnp.float32
    slopes = 2.0 ** (-8.0 * jnp.arange(1, N_HEADS + 1, dtype=f32) / N_HEADS)
    row = lambda v: v.reshape(1, -1).astype(f32)

    outs = []
    for bi in range(B):
        xs = x[bi]
        for l in range(depth):
            xb = xs.astype(bf16)
            w_in_l = w_in[l]
            qkv, = _matmul(xb, w_in_l, name="proj_qkv", n_cols=3 * ATTN_WIDTH, col_off=0, epilogue=_epi_plain,
                           out_dtypes=[bf16], tn=F32_WEIGHT_TN)
            xr, = _matmul(xb, w_in_l, name="proj_lru", n_cols=lru_width, col_off=3 * ATTN_WIDTH,
                          epilogue=_epi_plain, out_dtypes=[f32], tn=F32_WEIGHT_TN)
            gates, = _matmul(xb, w_in_l, name="proj_gates", n_cols=2 * D, col_off=3 * ATTN_WIDTH + lru_width,
                             epilogue=_epi_gate, out_dtypes=[bf16], rows=[(row(gate_b[l]), 0)], tn=F32_WEIGHT_TN)

            attn = _dilated_attention(qkv, slopes)

            lru = lambda wa, ba, wx, bx, lam: (wa[l].astype(bf16), wx[l].astype(bf16), row(ba[l]), row(bx[l]),
                                               row(lam[l]))
            h = _rg_lru(xr, conv_w[l], row(conv_b[l]),
                        lru(lru_wa_fwd, lru_ba_fwd, lru_wx_fwd, lru_bx_fwd, lru_lam_fwd),
                        lru(lru_wa_bwd, lru_ba_bwd, lru_wx_bwd, lru_bx_bwd, lru_lam_bwd))

            merged = _merge(attn, p_attn[l], h, p_lru[l], gates, tn=256)
            u, = _matmul(merged, w_out[l].astype(bf16), name="out_proj", n_cols=D, col_off=0,
                         epilogue=functools.partial(_epi_residual, alpha=alpha), out_dtypes=[f32], tiles=[xs])
            y, yb = _layer_norm(u, row(ln1_g[l]), row(ln1_b[l]), [f32, bf16])

            hid, = _matmul(yb, w_mlp1[l], name="mlp1", n_cols=w_mlp1.shape[-1], col_off=0,
                           epilogue=_epi_relu2, out_dtypes=[bf16], rows=[(row(b_mlp1[l]), 0)], tn=F32_WEIGHT_TN)
            u2 = _matmul_bias(hid, w_mlp2[l].astype(bf16), row(b_mlp2[l]), name="mlp2", tm=1024, tn=2048, tk=2048)
            xs, = _layer_norm(u2, row(ln2_g[l]), row(ln2_b[l]), [f32], res=y, alpha=alpha)
        outs.append(xs)
    return outs[0][None] if B == 1 else jnp.stack(outs, axis=0)
```

```python
import functools

import jax
import jax.numpy as jnp
from jax import lax
from jax.experimental import pallas as pl
from jax.experimental.pallas import tpu as pltpu

HEAD_DIM = 128
N_HEADS = 16
ATTN_WIDTH = N_HEADS * HEAD_DIM
DILATIONS = (1, 4, 16)
HALF = 64
LRU_BLOCK = 256
LRU_C = 8.0
LN_EPS = 1e-5
CONV_LEFT = 2

V7X_VMEM_LIMIT_BYTES = 56 * 1024 * 1024
SUBLANES = 8
BF16_SUBLANES = 16
LANES = 128

LRU_T = 512
LOG2_E = 1.4426950408889634
F32_TINY = 1e-37
QBLK = 128
KWIN = QBLK + 2 * HALF
ATT_CHUNK = 8
RESIDUE_SLOTS = 2
F32_WEIGHT_TN = 512


def _cparams(sem):
    return pltpu.CompilerParams(dimension_semantics=sem, vmem_limit_bytes=V7X_VMEM_LIMIT_BYTES)


def _mm_kernel(*refs, nk, n_extra, n_out, epilogue):
    a_ref, b_ref = refs[0], refs[1]
    extra = refs[2:2 + n_extra]
    outs = refs[2 + n_extra:2 + n_extra + n_out]
    prod = jnp.dot(a_ref[...], b_ref[...].astype(jnp.bfloat16), preferred_element_type=jnp.float32)
    if nk == 1:
        epilogue(prod, extra, outs)
        return
    acc_ref = refs[-1]
    k = pl.program_id(2)

    @pl.when(k == 0)
    def _():
        acc_ref[...] = jnp.zeros_like(acc_ref)

    acc_ref[...] += prod

    @pl.when(k == nk - 1)
    def _():
        epilogue(acc_ref[...], extra, outs)


def _matmul(a, b, *, name, n_cols, col_off, epilogue, out_dtypes, rows=(), tiles=(), tm=1024, tn=1024, tk=4096):
    M, K = a.shape
    tm, tn, tk = min(tm, M), min(tn, n_cols), min(tk, K)
    while col_off % tn or any(r_off % tn for _, r_off in rows):
        tn //= 2
    assert M % tm == 0 and n_cols % tn == 0 and K % tk == 0 and tn % LANES == 0
    nk = K // tk
    off = col_off // tn
    in_specs = [pl.BlockSpec((tm, tk), lambda i, j, k: (i, k)),
                pl.BlockSpec((tk, tn), lambda i, j, k: (k, j + off))]
    for r, r_off in rows:
        ro = r_off // tn
        in_specs.append(pl.BlockSpec((1, tn), lambda i, j, k, ro=ro: (0, j + ro)))
    for _ in tiles:
        in_specs.append(pl.BlockSpec((tm, tn), lambda i, j, k: (i, j)))
    out_specs = [pl.BlockSpec((tm, tn), lambda i, j, k: (i, j)) for _ in out_dtypes]
    out_shape = [jax.ShapeDtypeStruct((M, n_cols), dt) for dt in out_dtypes]
    n_extra = len(rows) + len(tiles)
    scratch = [pltpu.VMEM((tm, tn), jnp.float32)] if nk > 1 else []
    res = pl.pallas_call(
        functools.partial(_mm_kernel, nk=nk, n_extra=n_extra, n_out=len(out_dtypes), epilogue=epilogue),
        grid=(M // tm, n_cols // tn, nk),
        in_specs=in_specs, out_specs=out_specs, out_shape=out_shape, scratch_shapes=scratch,
        compiler_params=_cparams(("parallel", "parallel", "arbitrary")), name=name,
    )(a, b, *[r for r, _ in rows], *tiles)
    return res


def _mm_bias_kernel(a_ref, b_ref, bias_ref, o_ref):
    @pl.when(pl.program_id(2) == 0)
    def _():
        o_ref[...] = jnp.broadcast_to(bias_ref[...], o_ref.shape)

    o_ref[...] += jnp.dot(a_ref[...], b_ref[...], preferred_element_type=jnp.float32)


def _matmul_bias(a, b, bias, *, name, tm, tn, tk):
    M, K = a.shape
    N = b.shape[1]
    tm, tn, tk = min(tm, M), min(tn, N), min(tk, K)
    assert M % tm == 0 and N % tn == 0 and K % tk == 0
    return pl.pallas_call(
        _mm_bias_kernel,
        grid=(M // tm, N // tn, K // tk),
        in_specs=[pl.BlockSpec((tm, tk), lambda i, j, k: (i, k)),
                  pl.BlockSpec((tk, tn), lambda i, j, k: (k, j)),
                  pl.BlockSpec((1, tn), lambda i, j, k: (0, j))],
        out_specs=pl.BlockSpec((tm, tn), lambda i, j, k: (i, j)),
        out_shape=jax.ShapeDtypeStruct((M, N), jnp.float32),
        compiler_params=_cparams(("parallel", "parallel", "arbitrary")), name=name,
    )(a, b, bias)


def _epi_plain(acc, extra, outs):
    outs[0][...] = acc.astype(outs[0].dtype)


def _sigmoid(x):
    return 0.5 * jnp.tanh(0.5 * x) + 0.5


def _epi_gate(acc, extra, outs):
    outs[0][...] = _sigmoid(acc + extra[0][...]).astype(outs[0].dtype)


def _epi_relu2(acc, extra, outs):
    h = jnp.maximum(acc + extra[0][...], 0.0)
    outs[0][...] = (h * h).astype(outs[0].dtype)


def _epi_residual(acc, extra, outs, *, alpha):
    outs[0][...] = alpha * extra[0][...] + acc


def _merge_kernel(at_ref, pa_ref, h_ref, pl_ref, ga_ref, gb_ref, o_ref):
    pa = jnp.dot(at_ref[...], pa_ref[...].astype(jnp.bfloat16), preferred_element_type=jnp.float32)
    out = ga_ref[...].astype(jnp.float32) * pa
    pb = jnp.dot(h_ref[...], pl_ref[...].astype(jnp.bfloat16), preferred_element_type=jnp.float32)
    o_ref[...] = (out + gb_ref[...].astype(jnp.float32) * pb).astype(o_ref.dtype)


def _merge(attn, p_attn, h, p_lru, gates, *, tm=1024, tn=512):
    M, K1 = attn.shape
    K2 = h.shape[1]
    N = p_attn.shape[1]
    tm, tn = min(tm, M), min(tn, N)
    gb_off = N // tn
    return pl.pallas_call(
        _merge_kernel,
        grid=(M // tm, N // tn),
        in_specs=[
            pl.BlockSpec((tm, K1), lambda i, j: (i, 0)),
            pl.BlockSpec((K1, tn), lambda i, j: (0, j)),
            pl.BlockSpec((tm, K2), lambda i, j: (i, 0)),
            pl.BlockSpec((K2, tn), lambda i, j: (0, j)),
            pl.BlockSpec((tm, tn), lambda i, j: (i, j)),
            pl.BlockSpec((tm, tn), lambda i, j: (i, j + gb_off)),
        ],
        out_specs=pl.BlockSpec((tm, tn), lambda i, j: (i, j)),
        out_shape=jax.ShapeDtypeStruct((M, N), jnp.bfloat16),
        compiler_params=_cparams(("parallel", "arbitrary")), name="merge",
    )(attn, p_attn, h, p_lru, gates, gates)


def _ln_kernel(*refs, n_in, alpha):
    u = refs[0][...]
    if n_in == 2:
        u = alpha * refs[1][...] + u
    g_ref, b_ref = refs[n_in], refs[n_in + 1]
    mu = jnp.mean(u, axis=-1, keepdims=True)
    c = u - mu
    var = jnp.mean(c * c, axis=-1, keepdims=True)
    y = c * lax.rsqrt(var + LN_EPS) * g_ref[...] + b_ref[...]
    for o in refs[n_in + 2:]:
        o[...] = y.astype(o.dtype)


def _layer_norm(u, g, b, out_dtypes, *, res=None, alpha=1.0, tr=256):
    M, D = u.shape
    tr = min(tr, M)
    ins = [u] if res is None else [u, res]
    return pl.pallas_call(
        functools.partial(_ln_kernel, n_in=len(ins), alpha=alpha),
        grid=(M // tr,),
        in_specs=[pl.BlockSpec((tr, D), lambda i: (i, 0)) for _ in ins]
        + [pl.BlockSpec((1, D), lambda i: (0, 0)), pl.BlockSpec((1, D), lambda i: (0, 0))],
        out_specs=[pl.BlockSpec((tr, D), lambda i: (i, 0)) for _ in out_dtypes],
        out_shape=[jax.ShapeDtypeStruct((M, D), dt) for dt in out_dtypes],
        compiler_params=_cparams(("parallel",)), name="layer_norm",
    )(*ins, g, b)


def _band_chunk(q_ref, k_ref, v_ref, bias_ref, *, seq, chunk, scale, emit):
    n_sub = min(ATT_CHUNK, seq // QBLK)
    for jj in range(n_sub):
        q0 = (chunk * n_sub + jj) * QBLK
        if jj in (0, n_sub - 1):
            kstart = jnp.clip(q0 - HALF, 0, seq - KWIN)
            bias = bias_ref[jnp.where(q0 == 0, 1, jnp.where(q0 == seq - QBLK, 2, 0))]
        else:
            kstart = q0 - HALF
            bias = bias_ref[0]
        kstart = pl.multiple_of(kstart, HALF)
        q = q_ref[pl.ds(pl.multiple_of(q0, QBLK), QBLK), :]
        kw = k_ref[pl.ds(kstart, KWIN), :]
        vw = v_ref[pl.ds(kstart, KWIN), :]
        s = lax.dot_general(q, kw, (((1,), (1,)), ((), ())), preferred_element_type=jnp.float32)
        s = s * scale + bias
        m = jnp.max(s, axis=-1, keepdims=True)
        e = jnp.exp(s - m)
        den = jnp.sum(e, axis=-1, keepdims=True)
        o = jnp.dot(e.astype(jnp.bfloat16), vw, preferred_element_type=jnp.float32) / den
        emit(q0, o, m + jnp.log(den))


def _attn_kernel(slope_ref, q_ref, k_ref, v_ref, out_ref,
                 qf_ref, kf_ref, vf_ref, qd_ref, kd_ref, vd_ref, o_mid_ref, l_mid_ref, o_far_ref, l_far_ref,
                 bias_ref, *, scale):
    S = q_ref.shape[0]
    slope = slope_ref[pl.program_id(0)]
    col = lax.broadcasted_iota(jnp.int32, (QBLK, KWIN), 1)
    row = lax.broadcasted_iota(jnp.int32, (QBLK, KWIN), 0)
    delta = col - row

    def set_bias(dil):
        for case, off in enumerate((-HALF, 0, -2 * HALF)):
            dist = jnp.abs(delta + off)
            bias_ref[case] = jnp.where(dist <= HALF, dist.astype(jnp.float32) * (-slope * dil), -jnp.inf)

    piece = min(S, 1024)

    def widen(i, _):
        rows = pl.ds(pl.multiple_of(i * piece, piece), piece)
        qf_ref[rows, :] = q_ref[rows, :].astype(jnp.float32)
        kf_ref[rows, :] = k_ref[rows, :].astype(jnp.float32)
        vf_ref[rows, :] = v_ref[rows, :].astype(jnp.float32)
        return 0

    lax.fori_loop(0, S // piece, widen, 0)

    for dil, o_ref, l_ref in ((DILATIONS[2], o_far_ref, l_far_ref), (DILATIONS[1], o_mid_ref, l_mid_ref)):
        L = S // dil
        set_bias(dil)

        n_chunks = L // (min(ATT_CHUNK, L // QBLK) * QBLK)
        group = min(dil, RESIDUE_SLOTS) if n_chunks == 1 else 1

        def residues(i, _, dil=dil, L=L, o_ref=o_ref, l_ref=l_ref, n_chunks=n_chunks, group=group):
            for slot in range(group):
                r = i * group + slot
                gather = pl.ds(r, L, stride=dil)
                qd, kd, vd = qd_ref.at[slot], kd_ref.at[slot], vd_ref.at[slot]
                qd[pl.ds(0, L), :] = qf_ref[gather, :].astype(jnp.bfloat16)
                kd[pl.ds(0, L), :] = kf_ref[gather, :].astype(jnp.bfloat16)
                vd[pl.ds(0, L), :] = vf_ref[gather, :].astype(jnp.bfloat16)

                def emit(q0, o, lse, r=r):
                    scatter = pl.ds(q0 * dil + r, QBLK, stride=dil)
                    o_ref[scatter, :] = o
                    l_ref[scatter, :] = jnp.broadcast_to(lse, (QBLK, HEAD_DIM))

                def chunk(c, _, qd=qd, kd=kd, vd=vd, emit=emit):
                    _band_chunk(qd, kd, vd, bias_ref, seq=L, chunk=c, scale=scale, emit=emit)
                    return 0

                lax.fori_loop(0, n_chunks, chunk, 0)
            return 0

        lax.fori_loop(0, dil // group, residues, 0)

    set_bias(DILATIONS[0])

    def emit_mixed(q0, o_near, l_near):
        rows = pl.ds(pl.multiple_of(q0, QBLK), QBLK)
        l_mid, l_far = l_mid_ref[rows, :], l_far_ref[rows, :]
        m = jnp.maximum(jnp.maximum(l_mid, l_far), l_near)
        w_near, w_mid, w_far = jnp.exp(l_near - m), jnp.exp(l_mid - m), jnp.exp(l_far - m)
        mixed = w_near * o_near + w_mid * o_mid_ref[rows, :] + w_far * o_far_ref[rows, :]
        out_ref[rows, :] = (mixed / (w_near + w_mid + w_far)).astype(out_ref.dtype)

    def chunk_near(c, _):
        _band_chunk(q_ref, k_ref, v_ref, bias_ref, seq=S, chunk=c, scale=scale, emit=emit_mixed)
        return 0

    lax.fori_loop(0, S // (ATT_CHUNK * QBLK), chunk_near, 0)


def _dilated_attention(qkv, slopes):
    S = qkv.shape[0]
    assert DILATIONS[0] == 1 and S % (DILATIONS[2] * KWIN) == 0 and S % (ATT_CHUNK * QBLK) == 0
    f32, bf16 = jnp.float32, jnp.bfloat16
    l_mid = S // DILATIONS[1]
    head_col = lambda part: pl.BlockSpec((S, HEAD_DIM), lambda h: (0, part * N_HEADS + h))
    return pl.pallas_call(
        functools.partial(_attn_kernel, scale=HEAD_DIM ** -0.5),
        grid=(N_HEADS,),
        in_specs=[pl.BlockSpec(memory_space=pltpu.SMEM), head_col(0), head_col(1), head_col(2)],
        out_specs=pl.BlockSpec((S, HEAD_DIM), lambda h: (0, h)),
        out_shape=jax.ShapeDtypeStruct((S, ATTN_WIDTH), bf16),
        scratch_shapes=[pltpu.VMEM((S, HEAD_DIM), f32)] * 3
        + [pltpu.VMEM((RESIDUE_SLOTS, l_mid, HEAD_DIM), bf16)] * 3
        + [pltpu.VMEM((S, HEAD_DIM), f32)] * 4
        + [pltpu.VMEM((3, QBLK, KWIN), f32)],
        compiler_params=_cparams(("parallel",)), name="dilated_attention",
    )(slopes, qkv, qkv, qkv)


def _scan_local(a_slices, b_slices):
    hs, ps = [], []
    h = p = None
    for a, b in zip(a_slices, b_slices):
        h, p = (b, a) if h is None else (a * h + b, a * p)
        hs.append(h)
        ps.append(p)
    return hs, ps


def _slabs(ref, rows=None):
    parts = [ref[s] if rows is None else ref[s, rows, :] for s in range(ref.shape[0])]
    return jnp.concatenate(parts, axis=1) if len(parts) > 1 else parts[0]


def _to_slabs(ref, value, rows=None):
    for s in range(ref.shape[0]):
        piece = value[:, s * LANES:(s + 1) * LANES]
        if rows is None:
            ref[s] = piece
        else:
            ref[s, rows, :] = piece


def _lru_direction(x_ref, xp_ref, xn_ref, cw_ref, cb_ref, wa_ref, wx_ref, ba_ref, bx_ref, lam_ref,
                   carry_ref, ext_ref, t1a_ref, t1b_ref, e1_ref, g_ref, hs_ref, *, tb, n_tb, reverse):
    T, C = x_ref.shape
    n1 = T // SUBLANES
    n2 = n1 // SUBLANES
    strided = lambda k, n: pl.ds(k, n, stride=SUBLANES)
    _to_slabs(ext_ref, jnp.where(tb > 0, xp_ref[...], 0.0), pl.ds(0, SUBLANES))
    _to_slabs(ext_ref, x_ref[...], pl.ds(SUBLANES, T))
    _to_slabs(ext_ref, jnp.where(tb < n_tb - 1, xn_ref[...], 0.0), pl.ds(SUBLANES + T, SUBLANES))
    tap = {o: _slabs(ext_ref, strided(SUBLANES + o, n1)) for o in range(-CONV_LEFT, SUBLANES + 4 - CONV_LEFT - 1)}
    xk = []
    for k in range(SUBLANES):
        xck = cb_ref[...]
        for j in range(4):
            xck = xck + cw_ref[j:j + 1, :] * tap[k + j - CONV_LEFT]
        xk.append(xck)
    xk = jnp.concatenate(xk, axis=0)
    xkb = xk.astype(jnp.bfloat16)
    ga, gx = [], []
    for n in range(C // LRU_BLOCK):
        blk = xkb[:, n * LRU_BLOCK:(n + 1) * LRU_BLOCK]
        ga.append(jnp.dot(blk, wa_ref[n], preferred_element_type=jnp.float32))
        gx.append(jnp.dot(blk, wx_ref[n], preferred_element_type=jnp.float32))
    ga = jnp.concatenate(ga, axis=1) if len(ga) > 1 else ga[0]
    gx = jnp.concatenate(gx, axis=1) if len(gx) > 1 else gx[0]
    t_r = jnp.tanh(0.5 * ga + 0.5 * ba_ref[...])
    t_i = jnp.tanh(0.5 * gx + 0.5 * bx_ref[...])
    nlam = -lam_ref[...]
    softplus = jnp.maximum(nlam, 0.0) + jnp.log1p(jnp.exp(-jnp.abs(nlam)))
    half_rate = (-0.5 * LRU_C * LOG2_E) * softplus
    a = jnp.exp2(half_rate * t_r + half_rate)
    one_m_a2 = 1.0 - a * a
    root = one_m_a2 * lax.rsqrt(jnp.maximum(one_m_a2, F32_TINY))
    b = root * ((0.5 * t_i + 0.5) * xk)

    order = list(range(SUBLANES))[::-1] if reverse else list(range(SUBLANES))
    hs0, ps0 = _scan_local([a[k * n1:(k + 1) * n1] for k in order], [b[k * n1:(k + 1) * n1] for k in order])
    _to_slabs(t1a_ref, ps0[-1])
    _to_slabs(t1b_ref, hs0[-1])
    hs1, ps1 = _scan_local([_slabs(t1a_ref, strided(k, n2)) for k in order],
                           [_slabs(t1b_ref, strided(k, n2)) for k in order])
    grp_a, grp_b = ps1[-1], hs1[-1]
    carry = carry_ref[...]
    for g in (range(n2 - 1, -1, -1) if reverse else range(n2)):
        g_ref[g:g + 1, :] = carry
        carry = grp_a[g:g + 1, :] * carry + grp_b[g:g + 1, :]
    carry_ref[...] = carry
    enter_grp = g_ref[...]
    for j, k in enumerate(order):
        enter = enter_grp if j == 0 else hs1[j - 1] + ps1[j - 1] * enter_grp
        _to_slabs(e1_ref, enter, strided(k, n2))
    enter_tile = _slabs(e1_ref)
    for j, k in enumerate(order):
        _to_slabs(hs_ref, hs0[j] + ps0[j] * enter_tile, strided(k, n1))
    return _slabs(hs_ref)


def _lru_kernel(xf_ref, xfp_ref, xfn_ref, xb_ref, xbp_ref, xbn_ref, cw_ref, cb_ref,
                waf_ref, wxf_ref, baf_ref, bxf_ref, lamf_ref,
                wab_ref, wxb_ref, bab_ref, bxb_ref, lamb_ref,
                *rest, n_tb, n_round):
    w_in_refs, h_ref, w_out_refs = rest[:n_round], rest[n_round], rest[n_round + 1:2 * n_round + 1]
    carry_f, carry_b, *scan_scratch = rest[2 * n_round + 1:]
    s = pl.program_id(1)
    T = xf_ref.shape[0]
    for w_in_ref, w_out_ref in zip(w_in_refs, w_out_refs):
        w_out_ref[...] = w_in_ref[...].astype(w_out_ref.dtype)

    @pl.when(s == 0)
    def _():
        carry_f[...] = jnp.zeros_like(carry_f)
        carry_b[...] = jnp.zeros_like(carry_b)
        h_ref[...] = jnp.zeros_like(h_ref)

    def emit(rows, h):
        h_ref[rows, :] = (h_ref[rows, :].astype(jnp.float32) + h).astype(h_ref.dtype)

    hf = _lru_direction(xf_ref, xfp_ref, xfn_ref, cw_ref, cb_ref, waf_ref, wxf_ref, baf_ref, bxf_ref,
                        lamf_ref, carry_f, *scan_scratch, tb=s, n_tb=n_tb, reverse=False)
    emit(pl.ds(pl.multiple_of(s * T, T), T), hf)
    sb = n_tb - 1 - s
    hb = _lru_direction(xb_ref, xbp_ref, xbn_ref, cw_ref, cb_ref, wab_ref, wxb_ref, bab_ref, bxb_ref,
                        lamb_ref, carry_b, *scan_scratch, tb=sb, n_tb=n_tb, reverse=True)
    emit(pl.ds(pl.multiple_of(sb * T, T), T), hb)


def _rg_lru(xr, conv_w, conv_b, fwd, bwd, to_round, *, tc=512):
    S, C = xr.shape
    T, tc = LRU_T, min(tc, C)
    assert S % T == 0 and C % tc == 0 and tc % LRU_BLOCK == 0
    n_tb = S // T
    n_steps = (C // tc) * n_tb
    slab_rows = [w.shape[0] // n_steps for w in to_round]
    assert all(w.shape[0] % n_steps == 0 and r % BF16_SUBLANES == 0 for w, r in zip(to_round, slab_rows))
    slab_specs = [pl.BlockSpec((r, w.shape[1]), lambda c, s: (c * n_tb + s, 0)) for w, r in zip(to_round, slab_rows)]
    tpb = T // SUBLANES
    n_halo = S // SUBLANES
    nb = tc // LRU_BLOCK
    n_slab = tc // LANES
    n1 = T // SUBLANES

    def main(tb_of):
        return pl.BlockSpec((T, tc), lambda c, s: (tb_of(s), c))

    def prev(tb_of):
        return pl.BlockSpec((SUBLANES, tc), lambda c, s: (jnp.maximum(tb_of(s) * tpb - 1, 0), c))

    def nxt(tb_of):
        return pl.BlockSpec((SUBLANES, tc), lambda c, s: (jnp.minimum((tb_of(s) + 1) * tpb, n_halo - 1), c))

    fwd_tb = lambda s: s
    bwd_tb = lambda s: n_tb - 1 - s
    row = pl.BlockSpec((1, tc), lambda c, s: (0, c))
    wspec = pl.BlockSpec((nb, LRU_BLOCK, LRU_BLOCK), lambda c, s: (c, 0, 0))
    once = pl.Buffered(1)
    return pl.pallas_call(
        functools.partial(_lru_kernel, n_tb=n_tb, n_round=len(to_round)),
        grid=(C // tc, n_tb),
        in_specs=[main(fwd_tb), prev(fwd_tb), nxt(fwd_tb), main(bwd_tb), prev(bwd_tb), nxt(bwd_tb),
                  pl.BlockSpec((4, tc), lambda c, s: (0, c)), row,
                  wspec, wspec, row, row, row,
                  wspec, wspec, row, row, row] + slab_specs,
        out_specs=[pl.BlockSpec((S, tc), lambda c, s: (0, c), pipeline_mode=once)] + slab_specs,
        out_shape=[jax.ShapeDtypeStruct((S, C), jnp.bfloat16)]
        + [jax.ShapeDtypeStruct(w.shape, jnp.bfloat16) for w in to_round],
        scratch_shapes=[pltpu.VMEM((1, tc), jnp.float32), pltpu.VMEM((1, tc), jnp.float32),
                        pltpu.VMEM((n_slab, T + 2 * SUBLANES, LANES), jnp.float32),
                        pltpu.VMEM((n_slab, n1, LANES), jnp.float32),
                        pltpu.VMEM((n_slab, n1, LANES), jnp.float32),
                        pltpu.VMEM((n_slab, n1, LANES), jnp.float32),
                        pltpu.VMEM((n1 // SUBLANES, tc), jnp.float32),
                        pltpu.VMEM((n_slab, T, LANES), jnp.float32)],
        compiler_params=_cparams(("arbitrary", "arbitrary")), name="rg_lru",
    )(xr, xr, xr, xr, xr, xr, conv_w, conv_b, *fwd, *bwd, *to_round)


def kernel(x, w_in, gate_b, conv_w, conv_b, lru_wa_fwd, lru_ba_fwd, lru_wx_fwd, lru_bx_fwd, lru_lam_fwd, lru_wa_bwd, lru_ba_bwd, lru_wx_bwd, lru_bx_bwd, lru_lam_bwd, p_attn, p_lru, w_out, ln1_g, ln1_b, w_mlp1, b_mlp1, w_mlp2, b_mlp2, ln2_g, ln2_b):
    B, S, D = x.shape
    depth = w_in.shape[0]
    lru_width = conv_w.shape[-1]
    alpha = (2.0 * depth) ** 0.25
    bf16, f32 = jnp.bfloat16, jnp.float32
    slopes = 2.0 ** (-8.0 * jnp.arange(1, N_HEADS + 1, dtype=f32) / N_HEADS)
    row = lambda v: v.reshape(1, -1).astype(f32)

    outs = []
    for bi in range(B):
        xs = x[bi]
        for l in range(depth):
            xb = xs.astype(bf16)
            w_in_l = w_in[l]
            qkv, = _matmul(xb, w_in_l, name="proj_qkv", n_cols=3 * ATTN_WIDTH, col_off=0, epilogue=_epi_plain,
                           out_dtypes=[bf16], tn=F32_WEIGHT_TN)
            xr, = _matmul(xb, w_in_l, name="proj_lru", n_cols=lru_width, col_off=3 * ATTN_WIDTH,
                          epilogue=_epi_plain, out_dtypes=[f32], tn=F32_WEIGHT_TN)
            gates, = _matmul(xb, w_in_l, name="proj_gates", n_cols=2 * D, col_off=3 * ATTN_WIDTH + lru_width,
                             epilogue=_epi_gate, out_dtypes=[bf16], rows=[(row(gate_b[l]), 0)], tn=F32_WEIGHT_TN)

            attn = _dilated_attention(qkv, slopes)

            lru = lambda wa, ba, wx, bx, lam: (wa[l].astype(bf16), wx[l].astype(bf16), row(ba[l]), row(bx[l]),
                                               row(lam[l]))
            h, p_attn_b, p_lru_b, w_out_b, w_mlp1_b, w_mlp2_b = _rg_lru(
                xr, conv_w[l], row(conv_b[l]),
                lru(lru_wa_fwd, lru_ba_fwd, lru_wx_fwd, lru_bx_fwd, lru_lam_fwd),
                lru(lru_wa_bwd, lru_ba_bwd, lru_wx_bwd, lru_bx_bwd, lru_lam_bwd),
                [p_attn[l], p_lru[l], w_out[l], w_mlp1[l], w_mlp2[l]])

            merged = _merge(attn, p_attn_b, h, p_lru_b, gates)
            u, = _matmul(merged, w_out_b, name="out_proj", n_cols=D, col_off=0,
                         epilogue=functools.partial(_epi_residual, alpha=alpha), out_dtypes=[f32], tiles=[xs])
            y, yb = _layer_norm(u, row(ln1_g[l]), row(ln1_b[l]), [f32, bf16])

            hid, = _matmul(yb, w_mlp1_b, name="mlp1", n_cols=w_mlp1.shape[-1], col_off=0,
                           epilogue=_epi_relu2, out_dtypes=[bf16], rows=[(row(b_mlp1[l]), 0)])
            u2 = _matmul_bias(hid, w_mlp2_b, row(b_mlp2[l]), name="mlp2", tm=1024, tn=2048, tk=2048)
            xs, = _layer_norm(u2, row(ln2_g[l]), row(ln2_b[l]), [f32], res=y, alpha=alpha)
        outs.append(xs)
    return outs[0][None] if B == 1 else jnp.stack(outs, axis=0)
```

```python
import functools

import jax
import jax.numpy as jnp
from jax import lax
from jax.experimental import pallas as pl
from jax.experimental.pallas import tpu as pltpu

HEAD_DIM = 128
N_HEADS = 16
ATTN_WIDTH = N_HEADS * HEAD_DIM
DILATIONS = (1, 4, 16)
HALF = 64
LRU_BLOCK = 256
LRU_C = 8.0
LN_EPS = 1e-5
CONV_LEFT = 2

V7X_VMEM_LIMIT_BYTES = 56 * 1024 * 1024
SUBLANES = 8
BF16_SUBLANES = 16
LANES = 128

LRU_T = 512
LOG2_E = 1.4426950408889634
LN_2 = 0.6931471805599453
F32_TINY = 1e-37
QBLK = 128
KWIN = QBLK + 2 * HALF
ATT_PASSES = {1: 2, 4: 1, 16: 4}
F32_WEIGHT_TN = 512


def _cparams(sem):
    return pltpu.CompilerParams(dimension_semantics=sem, vmem_limit_bytes=V7X_VMEM_LIMIT_BYTES)


def _mm_kernel(*refs, nk, n_extra, n_out, epilogue):
    a_ref, b_ref = refs[0], refs[1]
    extra = refs[2:2 + n_extra]
    outs = refs[2 + n_extra:2 + n_extra + n_out]
    prod = jnp.dot(a_ref[...], b_ref[...].astype(jnp.bfloat16), preferred_element_type=jnp.float32)
    if nk == 1:
        epilogue(prod, extra, outs)
        return
    acc_ref = refs[-1]
    k = pl.program_id(2)

    @pl.when(k == 0)
    def _():
        acc_ref[...] = jnp.zeros_like(acc_ref)

    acc_ref[...] += prod

    @pl.when(k == nk - 1)
    def _():
        epilogue(acc_ref[...], extra, outs)


def _matmul(a, b, *, name, n_cols, col_off, epilogue, out_dtypes, rows=(), tm=1024, tn=1024, tk=4096):
    M, K = a.shape
    tm, tn, tk = min(tm, M), min(tn, n_cols), min(tk, K)
    while col_off % tn or any(r_off % tn for _, r_off in rows):
        tn //= 2
    assert M % tm == 0 and n_cols % tn == 0 and K % tk == 0 and tn % LANES == 0
    nk = K // tk
    off = col_off // tn
    in_specs = [pl.BlockSpec((tm, tk), lambda i, j, k: (i, k)),
                pl.BlockSpec((tk, tn), lambda i, j, k: (k, j + off))]
    for r, r_off in rows:
        ro = r_off // tn
        in_specs.append(pl.BlockSpec((1, tn), lambda i, j, k, ro=ro: (0, j + ro)))
    out_specs = [pl.BlockSpec((tm, tn), lambda i, j, k: (i, j)) for _ in out_dtypes]
    out_shape = [jax.ShapeDtypeStruct((M, n_cols), dt) for dt in out_dtypes]
    scratch = [pltpu.VMEM((tm, tn), jnp.float32)] if nk > 1 else []
    res = pl.pallas_call(
        functools.partial(_mm_kernel, nk=nk, n_extra=len(rows), n_out=len(out_dtypes), epilogue=epilogue),
        grid=(M // tm, n_cols // tn, nk),
        in_specs=in_specs, out_specs=out_specs, out_shape=out_shape, scratch_shapes=scratch,
        compiler_params=_cparams(("parallel", "parallel", "arbitrary")), name=name,
    )(a, b, *[r for r, _ in rows])
    return res


def _mm_bias_kernel(a_ref, b_ref, bias_ref, o_ref):
    @pl.when(pl.program_id(2) == 0)
    def _():
        o_ref[...] = jnp.broadcast_to(bias_ref[...], o_ref.shape)

    o_ref[...] += jnp.dot(a_ref[...], b_ref[...], preferred_element_type=jnp.float32)


def _matmul_bias(a, b, bias, *, name, tm, tn, tk):
    M, K = a.shape
    N = b.shape[1]
    tm, tn, tk = min(tm, M), min(tn, N), min(tk, K)
    assert M % tm == 0 and N % tn == 0 and K % tk == 0
    return pl.pallas_call(
        _mm_bias_kernel,
        grid=(M // tm, N // tn, K // tk),
        in_specs=[pl.BlockSpec((tm, tk), lambda i, j, k: (i, k)),
                  pl.BlockSpec((tk, tn), lambda i, j, k: (k, j)),
                  pl.BlockSpec((1, tn), lambda i, j, k: (0, j))],
        out_specs=pl.BlockSpec((tm, tn), lambda i, j, k: (i, j)),
        out_shape=jax.ShapeDtypeStruct((M, N), jnp.float32),
        compiler_params=_cparams(("parallel", "parallel", "arbitrary")), name=name,
    )(a, b, bias)


def _epi_plain(acc, extra, outs):
    outs[0][...] = acc.astype(outs[0].dtype)


def _sigmoid(x):
    return 0.5 * jnp.tanh(0.5 * x) + 0.5


def _epi_gate(acc, extra, outs):
    outs[0][...] = _sigmoid(acc + extra[0][...]).astype(outs[0].dtype)


def _epi_relu2(acc, extra, outs):
    h = jnp.maximum(acc + extra[0][...], 0.0)
    outs[0][...] = (h * h).astype(outs[0].dtype)


def _merge_kernel(at_ref, pa_ref, h_ref, pl_ref, ga_ref, gb_ref, o_ref):
    pa = jnp.dot(at_ref[...], pa_ref[...], preferred_element_type=jnp.float32)
    out = ga_ref[...].astype(jnp.float32) * pa
    pb = jnp.dot(h_ref[...], pl_ref[...], preferred_element_type=jnp.float32)
    o_ref[...] = (out + gb_ref[...].astype(jnp.float32) * pb).astype(o_ref.dtype)


def _merge(attn, p_attn, h, p_lru, gates, *, tm=1024, tn=512):
    M, K1 = attn.shape
    K2 = h.shape[1]
    N = p_attn.shape[1]
    tm, tn = min(tm, M), min(tn, N)
    gb_off = N // tn
    return pl.pallas_call(
        _merge_kernel,
        grid=(M // tm, N // tn),
        in_specs=[
            pl.BlockSpec((tm, K1), lambda i, j: (i, 0)),
            pl.BlockSpec((K1, tn), lambda i, j: (0, j)),
            pl.BlockSpec((tm, K2), lambda i, j: (i, 0)),
            pl.BlockSpec((K2, tn), lambda i, j: (0, j)),
            pl.BlockSpec((tm, tn), lambda i, j: (i, j)),
            pl.BlockSpec((tm, tn), lambda i, j: (i, j + gb_off)),
        ],
        out_specs=pl.BlockSpec((tm, tn), lambda i, j: (i, j)),
        out_shape=jax.ShapeDtypeStruct((M, N), jnp.bfloat16),
        compiler_params=_cparams(("parallel", "arbitrary")), name="merge",
    )(attn, p_attn, h, p_lru, gates, gates)


def _ln_rows(u, g_ref, b_ref, outs):
    mu = jnp.mean(u, axis=-1, keepdims=True)
    c = u - mu
    var = jnp.mean(c * c, axis=-1, keepdims=True)
    y = c * lax.rsqrt(var + LN_EPS) * g_ref[...] + b_ref[...]
    for o in outs:
        o[...] = y.astype(o.dtype)


def _res_ln_kernel(u_ref, res_ref, g_ref, b_ref, *outs, alpha):
    _ln_rows(alpha * res_ref[...] + u_ref[...], g_ref, b_ref, outs)


def _residual_layer_norm(u, res, g, b, out_dtypes, *, alpha, tr=256):
    M, D = u.shape
    tr = min(tr, M)
    rows = pl.BlockSpec((tr, D), lambda i: (i, 0))
    vec = pl.BlockSpec((1, D), lambda i: (0, 0))
    return pl.pallas_call(
        functools.partial(_res_ln_kernel, alpha=alpha),
        grid=(M // tr,),
        in_specs=[rows, rows, vec, vec],
        out_specs=[rows for _ in out_dtypes],
        out_shape=[jax.ShapeDtypeStruct((M, D), dt) for dt in out_dtypes],
        compiler_params=_cparams(("parallel",)), name="layer_norm",
    )(u, res, g, b)


def _proj_ln_kernel(a_ref, w_ref, res_ref, g_ref, b_ref, *outs, alpha):
    u = alpha * res_ref[...] + jnp.dot(a_ref[...], w_ref[...], preferred_element_type=jnp.float32)
    _ln_rows(u, g_ref, b_ref, outs)


def _proj_layer_norm(a, w, res, g, b, out_dtypes, *, alpha, tm=128):
    M, K = a.shape
    N = w.shape[1]
    tm = min(tm, M)
    rows = lambda n: pl.BlockSpec((tm, n), lambda i: (i, 0))
    fixed = lambda shape, **kw: pl.BlockSpec(shape, lambda i: (0, 0), **kw)
    return pl.pallas_call(
        functools.partial(_proj_ln_kernel, alpha=alpha),
        grid=(M // tm,),
        in_specs=[rows(K), fixed((K, N), pipeline_mode=pl.Buffered(1)), rows(N), fixed((1, N)), fixed((1, N))],
        out_specs=[rows(N) for _ in out_dtypes],
        out_shape=[jax.ShapeDtypeStruct((M, N), dt) for dt in out_dtypes],
        compiler_params=_cparams(("parallel",)), name="out_proj_ln",
    )(a, w, res, g, b)


def _band_block(q_ref, k_ref, v_ref, bias_ref, *, base, seq, q0, interior, scale, emit):
    if interior:
        case, kstart = 0, q0 - HALF
    else:
        case = jnp.where(q0 == 0, 1, jnp.where(q0 == seq - QBLK, 2, 0))
        kstart = jnp.clip(q0 - HALF, 0, seq - KWIN)
    q = q_ref[pl.ds(pl.multiple_of(base + q0, QBLK), QBLK), :]
    kw = k_ref[pl.ds(pl.multiple_of(base + kstart, HALF), KWIN), :]
    vw = v_ref[pl.ds(pl.multiple_of(base + kstart, HALF), KWIN), :]
    s = lax.dot_general(q, kw, (((1,), (1,)), ((), ())), preferred_element_type=jnp.float32)
    s = s * (scale * LOG2_E) + bias_ref[case]
    m = jnp.max(s, axis=-1, keepdims=True)
    e = jnp.exp2(s - m)
    den = jnp.sum(e, axis=-1, keepdims=True)
    o = jnp.dot(e.astype(jnp.bfloat16), vw, preferred_element_type=jnp.float32) / den
    emit(o, m * LN_2 + jnp.log(den))


def _attn_kernel(slope_ref, q_ref, k_ref, v_ref, out_ref,
                 qf_ref, kf_ref, vf_ref, qd_ref, kd_ref, vd_ref, o_mid_ref, l_mid_ref, o_far_ref, l_far_ref,
                 bias_ref, *, scale):
    S = q_ref.shape[0]
    slope = slope_ref[pl.program_id(0)]
    col = lax.broadcasted_iota(jnp.int32, (QBLK, KWIN), 1)
    row = lax.broadcasted_iota(jnp.int32, (QBLK, KWIN), 0)
    delta = col - row

    def set_bias(dil):
        for case, off in enumerate((-HALF, 0, -2 * HALF)):
            dist = jnp.abs(delta + off)
            bias_ref[case] = jnp.where(dist <= HALF, dist.astype(jnp.float32) * (-slope * dil * LOG2_E), -jnp.inf)

    n_widen = 2
    piece = S // n_widen

    def widen(i, _):
        rows = pl.ds(pl.multiple_of(i * piece, piece), piece)
        qf_ref[rows, :] = q_ref[rows, :].astype(jnp.float32)
        kf_ref[rows, :] = k_ref[rows, :].astype(jnp.float32)
        vf_ref[rows, :] = v_ref[rows, :].astype(jnp.float32)
        return 0

    lax.fori_loop(0, n_widen, widen, 0)

    for dil, o_ref, l_ref in ((DILATIONS[2], o_far_ref, l_far_ref), (DILATIONS[1], o_mid_ref, l_mid_ref)):
        L = S // dil
        group = dil // ATT_PASSES[dil]
        set_bias(dil)

        def residues(i, _, dil=dil, L=L, group=group, o_ref=o_ref, l_ref=l_ref):
            for g in range(group):
                r = i * group + g
                gather, rows = pl.ds(r, L, stride=dil), pl.ds(pl.multiple_of(r * L, L), L)
                qd_ref[rows, :] = qf_ref[gather, :].astype(jnp.bfloat16)
                kd_ref[rows, :] = kf_ref[gather, :].astype(jnp.bfloat16)
                vd_ref[rows, :] = vf_ref[gather, :].astype(jnp.bfloat16)
                for q0 in range(0, L, QBLK):
                    def emit(o, lse, scatter=pl.ds(q0 * dil + r, QBLK, stride=dil)):
                        o_ref[scatter, :] = o
                        l_ref[scatter, :] = jnp.broadcast_to(lse, (QBLK, HEAD_DIM))

                    _band_block(qd_ref, kd_ref, vd_ref, bias_ref, base=r * L, seq=L, q0=q0,
                                interior=0 < q0 < L - QBLK, scale=scale, emit=emit)
            return 0

        lax.fori_loop(0, ATT_PASSES[dil], residues, 0)

    set_bias(DILATIONS[0])
    n_blk = S // QBLK // ATT_PASSES[1]

    def near(i, _):
        for j in range(n_blk):
            q0 = (i * n_blk + j) * QBLK

            def emit_mixed(o_near, l_near, rows=pl.ds(pl.multiple_of(q0, QBLK), QBLK)):
                l_mid, l_far = l_mid_ref[rows, :], l_far_ref[rows, :]
                m = jnp.maximum(jnp.maximum(l_mid, l_far), l_near)
                w_near, w_mid, w_far = jnp.exp(l_near - m), jnp.exp(l_mid - m), jnp.exp(l_far - m)
                mixed = w_near * o_near + w_mid * o_mid_ref[rows, :] + w_far * o_far_ref[rows, :]
                out_ref[rows, :] = (mixed / (w_near + w_mid + w_far)).astype(out_ref.dtype)

            _band_block(q_ref, k_ref, v_ref, bias_ref, base=0, seq=S, q0=q0, interior=0 < j < n_blk - 1,
                        scale=scale, emit=emit_mixed)
        return 0

    lax.fori_loop(0, ATT_PASSES[1], near, 0)


def _dilated_attention(qkv, slopes):
    S = qkv.shape[0]
    assert DILATIONS[0] == 1 and S % (DILATIONS[2] * KWIN) == 0 and S % (ATT_PASSES[1] * QBLK) == 0
    f32, bf16 = jnp.float32, jnp.bfloat16
    head_col = lambda part: pl.BlockSpec((S, HEAD_DIM), lambda h: (0, part * N_HEADS + h))
    return pl.pallas_call(
        functools.partial(_attn_kernel, scale=HEAD_DIM ** -0.5),
        grid=(N_HEADS,),
        in_specs=[pl.BlockSpec(memory_space=pltpu.SMEM), head_col(0), head_col(1), head_col(2)],
        out_specs=pl.BlockSpec((S, HEAD_DIM), lambda h: (0, h)),
        out_shape=jax.ShapeDtypeStruct((S, ATTN_WIDTH), bf16),
        scratch_shapes=[pltpu.VMEM((S, HEAD_DIM), f32)] * 3
        + [pltpu.VMEM((S, HEAD_DIM), bf16)] * 3
        + [pltpu.VMEM((S, HEAD_DIM), f32)] * 4
        + [pltpu.VMEM((3, QBLK, KWIN), f32)],
        compiler_params=_cparams(("parallel",)), name="dilated_attention",
    )(slopes, qkv, qkv, qkv)


def _scan_local(a_slices, b_slices):
    hs, ps = [], []
    h = p = None
    for a, b in zip(a_slices, b_slices):
        h, p = (b, a) if h is None else (a * h + b, a * p)
        hs.append(h)
        ps.append(p)
    return hs, ps


def _slabs(ref, rows=None):
    parts = [ref[s] if rows is None else ref[s, rows, :] for s in range(ref.shape[0])]
    return jnp.concatenate(parts, axis=1) if len(parts) > 1 else parts[0]


def _to_slabs(ref, value, rows=None):
    for s in range(ref.shape[0]):
        piece = value[:, s * LANES:(s + 1) * LANES]
        if rows is None:
            ref[s] = piece
        else:
            ref[s, rows, :] = piece


def _lru_direction(x_ref, xp_ref, xn_ref, cw_ref, cb_ref, wa_ref, wx_ref, ba_ref, bx_ref, lam_ref,
                   carry_ref, ext_ref, t1a_ref, t1b_ref, e1_ref, g_ref, hs_ref, *, tb, n_tb, reverse):
    T, C = x_ref.shape
    n1 = T // SUBLANES
    n2 = n1 // SUBLANES
    strided = lambda k, n: pl.ds(k, n, stride=SUBLANES)
    _to_slabs(ext_ref, jnp.where(tb > 0, xp_ref[...], 0.0), pl.ds(0, SUBLANES))
    _to_slabs(ext_ref, x_ref[...], pl.ds(SUBLANES, T))
    _to_slabs(ext_ref, jnp.where(tb < n_tb - 1, xn_ref[...], 0.0), pl.ds(SUBLANES + T, SUBLANES))
    tap = {o: _slabs(ext_ref, strided(SUBLANES + o, n1)) for o in range(-CONV_LEFT, SUBLANES + 4 - CONV_LEFT - 1)}
    xk = []
    for k in range(SUBLANES):
        xck = cb_ref[...]
        for j in range(4):
            xck = xck + cw_ref[j:j + 1, :] * tap[k + j - CONV_LEFT]
        xk.append(xck)
    xk = jnp.concatenate(xk, axis=0)
    xkb = xk.astype(jnp.bfloat16)
    ga, gx = [], []
    for n in range(C // LRU_BLOCK):
        blk = xkb[:, n * LRU_BLOCK:(n + 1) * LRU_BLOCK]
        ga.append(jnp.dot(blk, wa_ref[n], preferred_element_type=jnp.float32))
        gx.append(jnp.dot(blk, wx_ref[n], preferred_element_type=jnp.float32))
    ga = jnp.concatenate(ga, axis=1) if len(ga) > 1 else ga[0]
    gx = jnp.concatenate(gx, axis=1) if len(gx) > 1 else gx[0]
    t_r = jnp.tanh(0.5 * ga + 0.5 * ba_ref[...])
    t_i = jnp.tanh(0.5 * gx + 0.5 * bx_ref[...])
    nlam = -lam_ref[...]
    softplus = jnp.maximum(nlam, 0.0) + jnp.log1p(jnp.exp(-jnp.abs(nlam)))
    half_rate = (-0.5 * LRU_C * LOG2_E) * softplus
    a = jnp.exp2(half_rate * t_r + half_rate)
    one_m_a2 = 1.0 - a * a
    root = one_m_a2 * lax.rsqrt(jnp.maximum(one_m_a2, F32_TINY))
    b = root * ((0.5 * t_i + 0.5) * xk)

    order = list(range(SUBLANES))[::-1] if reverse else list(range(SUBLANES))
    hs0, ps0 = _scan_local([a[k * n1:(k + 1) * n1] for k in order], [b[k * n1:(k + 1) * n1] for k in order])
    _to_slabs(t1a_ref, ps0[-1])
    _to_slabs(t1b_ref, hs0[-1])
    hs1, ps1 = _scan_local([_slabs(t1a_ref, strided(k, n2)) for k in order],
                           [_slabs(t1b_ref, strided(k, n2)) for k in order])
    grp_a, grp_b = ps1[-1], hs1[-1]
    carry = carry_ref[...]
    for g in (range(n2 - 1, -1, -1) if reverse else range(n2)):
        g_ref[g:g + 1, :] = carry
        carry = grp_a[g:g + 1, :] * carry + grp_b[g:g + 1, :]
    carry_ref[...] = carry
    enter_grp = g_ref[...]
    for j, k in enumerate(order):
        enter = enter_grp if j == 0 else hs1[j - 1] + ps1[j - 1] * enter_grp
        _to_slabs(e1_ref, enter, strided(k, n2))
    enter_tile = _slabs(e1_ref)
    for j, k in enumerate(order):
        _to_slabs(hs_ref, hs0[j] + ps0[j] * enter_tile, strided(k, n1))
    return _slabs(hs_ref)


def _lru_kernel(xf_ref, xfp_ref, xfn_ref, xb_ref, xbp_ref, xbn_ref, cw_ref, cb_ref,
                waf_ref, wxf_ref, baf_ref, bxf_ref, lamf_ref,
                wab_ref, wxb_ref, bab_ref, bxb_ref, lamb_ref,
                *rest, n_tb, n_round):
    w_in_refs, h_ref, w_out_refs = rest[:n_round], rest[n_round], rest[n_round + 1:2 * n_round + 1]
    carry_f, carry_b, *scan_scratch = rest[2 * n_round + 1:]
    s = pl.program_id(1)
    T = xf_ref.shape[0]
    for w_in_ref, w_out_ref in zip(w_in_refs, w_out_refs):
        w_out_ref[...] = w_in_ref[...].astype(w_out_ref.dtype)

    @pl.when(s == 0)
    def _():
        carry_f[...] = jnp.zeros_like(carry_f)
        carry_b[...] = jnp.zeros_like(carry_b)
        h_ref[...] = jnp.zeros_like(h_ref)

    def emit(rows, h):
        h_ref[rows, :] = (h_ref[rows, :].astype(jnp.float32) + h).astype(h_ref.dtype)

    hf = _lru_direction(xf_ref, xfp_ref, xfn_ref, cw_ref, cb_ref, waf_ref, wxf_ref, baf_ref, bxf_ref,
                        lamf_ref, carry_f, *scan_scratch, tb=s, n_tb=n_tb, reverse=False)
    emit(pl.ds(pl.multiple_of(s * T, T), T), hf)
    sb = n_tb - 1 - s
    hb = _lru_direction(xb_ref, xbp_ref, xbn_ref, cw_ref, cb_ref, wab_ref, wxb_ref, bab_ref, bxb_ref,
                        lamb_ref, carry_b, *scan_scratch, tb=sb, n_tb=n_tb, reverse=True)
    emit(pl.ds(pl.multiple_of(sb * T, T), T), hb)


def _rg_lru(xr, conv_w, conv_b, fwd, bwd, to_round, *, tc=512):
    S, C = xr.shape
    T, tc = LRU_T, min(tc, C)
    assert S % T == 0 and C % tc == 0 and tc % LRU_BLOCK == 0
    n_tb = S // T
    n_steps = (C // tc) * n_tb
    slab_rows = [w.shape[0] // n_steps for w in to_round]
    assert all(w.shape[0] % n_steps == 0 and r % BF16_SUBLANES == 0 for w, r in zip(to_round, slab_rows))
    slab_specs = [pl.BlockSpec((r, w.shape[1]), lambda c, s: (c * n_tb + s, 0)) for w, r in zip(to_round, slab_rows)]
    tpb = T // SUBLANES
    n_halo = S // SUBLANES
    nb = tc // LRU_BLOCK
    n_slab = tc // LANES
    n1 = T // SUBLANES

    def main(tb_of):
        return pl.BlockSpec((T, tc), lambda c, s: (tb_of(s), c))

    def prev(tb_of):
        return pl.BlockSpec((SUBLANES, tc), lambda c, s: (jnp.maximum(tb_of(s) * tpb - 1, 0), c))

    def nxt(tb_of):
        return pl.BlockSpec((SUBLANES, tc), lambda c, s: (jnp.minimum((tb_of(s) + 1) * tpb, n_halo - 1), c))

    fwd_tb = lambda s: s
    bwd_tb = lambda s: n_tb - 1 - s
    row = pl.BlockSpec((1, tc), lambda c, s: (0, c))
    wspec = pl.BlockSpec((nb, LRU_BLOCK, LRU_BLOCK), lambda c, s: (c, 0, 0))
    once = pl.Buffered(1)
    return pl.pallas_call(
        functools.partial(_lru_kernel, n_tb=n_tb, n_round=len(to_round)),
        grid=(C // tc, n_tb),
        in_specs=[main(fwd_tb), prev(fwd_tb), nxt(fwd_tb), main(bwd_tb), prev(bwd_tb), nxt(bwd_tb),
                  pl.BlockSpec((4, tc), lambda c, s: (0, c)), row,
                  wspec, wspec, row, row, row,
                  wspec, wspec, row, row, row] + slab_specs,
        out_specs=[pl.BlockSpec((S, tc), lambda c, s: (0, c), pipeline_mode=once)] + slab_specs,
        out_shape=[jax.ShapeDtypeStruct((S, C), jnp.bfloat16)]
        + [jax.ShapeDtypeStruct(w.shape, jnp.bfloat16) for w in to_round],
        scratch_shapes=[pltpu.VMEM((1, tc), jnp.float32), pltpu.VMEM((1, tc), jnp.float32),
                        pltpu.VMEM((n_slab, T + 2 * SUBLANES, LANES), jnp.float32),
                        pltpu.VMEM((n_slab, n1, LANES), jnp.float32),
                        pltpu.VMEM((n_slab, n1, LANES), jnp.float32),
                        pltpu.VMEM((n_slab, n1, LANES), jnp.float32),
                        pltpu.VMEM((n1 // SUBLANES, tc), jnp.float32),
                        pltpu.VMEM((n_slab, T, LANES), jnp.float32)],
        compiler_params=_cparams(("arbitrary", "arbitrary")), name="rg_lru",
    )(xr, xr, xr, xr, xr, xr, conv_w, conv_b, *fwd, *bwd, *to_round)


def kernel(x, w_in, gate_b, conv_w, conv_b, lru_wa_fwd, lru_ba_fwd, lru_wx_fwd, lru_bx_fwd, lru_lam_fwd, lru_wa_bwd, lru_ba_bwd, lru_wx_bwd, lru_bx_bwd, lru_lam_bwd, p_attn, p_lru, w_out, ln1_g, ln1_b, w_mlp1, b_mlp1, w_mlp2, b_mlp2, ln2_g, ln2_b):
    B, S, D = x.shape
    depth = w_in.shape[0]
    lru_width = conv_w.shape[-1]
    alpha = (2.0 * depth) ** 0.25
    bf16, f32 = jnp.bfloat16, jnp.float32
    slopes = 2.0 ** (-8.0 * jnp.arange(1, N_HEADS + 1, dtype=f32) / N_HEADS)
    row = lambda v: v.reshape(1, -1).astype(f32)

    outs = []
    for bi in range(B):
        xs = x[bi]
        for l in range(depth):
            xb = xs.astype(bf16)
            w_in_l = w_in[l]
            qkv, = _matmul(xb, w_in_l, name="proj_qkv", n_cols=3 * ATTN_WIDTH, col_off=0, epilogue=_epi_plain,
                           out_dtypes=[bf16], tn=F32_WEIGHT_TN)
            xr, = _matmul(xb, w_in_l, name="proj_lru", n_cols=lru_width, col_off=3 * ATTN_WIDTH,
                          epilogue=_epi_plain, out_dtypes=[f32], tn=F32_WEIGHT_TN)
            gates, = _matmul(xb, w_in_l, name="proj_gates", n_cols=2 * D, col_off=3 * ATTN_WIDTH + lru_width,
                             epilogue=_epi_gate, out_dtypes=[bf16], rows=[(row(gate_b[l]), 0)], tn=F32_WEIGHT_TN)

            attn = _dilated_attention(qkv, slopes)

            lru = lambda wa, ba, wx, bx, lam: (wa[l].astype(bf16), wx[l].astype(bf16), row(ba[l]), row(bx[l]),
                                               row(lam[l]))
            h, p_attn_b, p_lru_b, w_out_b, w_mlp1_b, w_mlp2_b = _rg_lru(
                xr, conv_w[l], row(conv_b[l]),
                lru(lru_wa_fwd, lru_ba_fwd, lru_wx_fwd, lru_bx_fwd, lru_lam_fwd),
                lru(lru_wa_bwd, lru_ba_bwd, lru_wx_bwd, lru_bx_bwd, lru_lam_bwd),
                [p_attn[l], p_lru[l], w_out[l], w_mlp1[l], w_mlp2[l]])

            merged = _merge(attn, p_attn_b, h, p_lru_b, gates)
            y, yb = _proj_layer_norm(merged, w_out_b, xs, row(ln1_g[l]), row(ln1_b[l]), [f32, bf16], alpha=alpha)

            hid, = _matmul(yb, w_mlp1_b, name="mlp1", n_cols=w_mlp1.shape[-1], col_off=0,
                           epilogue=_epi_relu2, out_dtypes=[bf16], rows=[(row(b_mlp1[l]), 0)])
            u2 = _matmul_bias(hid, w_mlp2_b, row(b_mlp2[l]), name="mlp2", tm=1024, tn=2048, tk=2048)
            xs, = _residual_layer_norm(u2, y, row(ln2_g[l]), row(ln2_b[l]), [f32], alpha=alpha)
        outs.append(xs)
    return outs[0][None] if B == 1 else jnp.stack(outs, axis=0)
```

```python
import functools

import jax
import jax.numpy as jnp
from jax import lax
from jax.experimental import pallas as pl
from jax.experimental.pallas import tpu as pltpu

HEAD_DIM = 128
N_HEADS = 16
ATTN_WIDTH = N_HEADS * HEAD_DIM
DILATIONS = (1, 4, 16)
HALF = 64
LRU_BLOCK = 256
LRU_C = 8.0
LN_EPS = 1e-5
CONV_LEFT = 2

V7X_VMEM_LIMIT_BYTES = 56 * 1024 * 1024
SUBLANES = 8
BF16_SUBLANES = 16
LANES = 128

LRU_T = 512
LOG2_E = 1.4426950408889634
LN_2 = 0.6931471805599453
F32_TINY = 1e-37
QBLK = 128
KWIN = QBLK + 2 * HALF
ATT_PASSES = {1: 2, 4: 1, 16: 4}
F32_WEIGHT_TN = 512


def _cparams(sem):
    return pltpu.CompilerParams(dimension_semantics=sem, vmem_limit_bytes=V7X_VMEM_LIMIT_BYTES)


def _mm_kernel(*refs, nk, n_extra, n_out, n_round, epilogue):
    a_ref, b_ref = refs[0], refs[1]
    extra = refs[2:2 + n_extra]
    outs = refs[2 + n_extra + n_round:2 + n_extra + n_round + n_out]
    for src, dst in zip(refs[2 + n_extra:2 + n_extra + n_round], refs[2 + n_extra + n_round + n_out:]):
        dst[...] = src[...].astype(dst.dtype)
    prod = jnp.dot(a_ref[...], b_ref[...].astype(jnp.bfloat16), preferred_element_type=jnp.float32)
    if nk == 1:
        epilogue(prod, extra, outs)
        return
    acc_ref = refs[-1]
    k = pl.program_id(2)

    @pl.when(k == 0)
    def _():
        acc_ref[...] = jnp.zeros_like(acc_ref)

    acc_ref[...] += prod

    @pl.when(k == nk - 1)
    def _():
        epilogue(acc_ref[...], extra, outs)


def _matmul(a, b, *, name, n_cols, col_off, epilogue, out_dtypes, rows=(), to_round=(), tm=1024, tn=1024,
            tk=4096):
    M, K = a.shape
    tm, tn, tk = min(tm, M), min(tn, n_cols), min(tk, K)
    while col_off % tn or any(r_off % tn for _, r_off in rows):
        tn //= 2
    assert M % tm == 0 and n_cols % tn == 0 and K % tk == 0 and tn % LANES == 0
    nk = K // tk
    nj = n_cols // tn
    off = col_off // tn
    in_specs = [pl.BlockSpec((tm, tk), lambda i, j, k: (i, k)),
                pl.BlockSpec((tk, tn), lambda i, j, k: (k, j + off))]
    for r, r_off in rows:
        ro = r_off // tn
        in_specs.append(pl.BlockSpec((1, tn), lambda i, j, k, ro=ro: (0, j + ro)))
    n_steps = (M // tm) * nj
    assert not to_round or nk == 1
    assert all(w.shape[0] % (n_steps * BF16_SUBLANES) == 0 for w in to_round)
    slab_specs = [pl.BlockSpec((w.shape[0] // n_steps, w.shape[1]), lambda i, j, k: (i * nj + j, 0))
                  for w in to_round]
    out_specs = [pl.BlockSpec((tm, tn), lambda i, j, k: (i, j)) for _ in out_dtypes]
    out_shape = [jax.ShapeDtypeStruct((M, n_cols), dt) for dt in out_dtypes]
    scratch = [pltpu.VMEM((tm, tn), jnp.float32)] if nk > 1 else []
    res = pl.pallas_call(
        functools.partial(_mm_kernel, nk=nk, n_extra=len(rows), n_out=len(out_dtypes), n_round=len(to_round),
                          epilogue=epilogue),
        grid=(M // tm, nj, nk),
        in_specs=in_specs + slab_specs, out_specs=out_specs + slab_specs,
        out_shape=out_shape + [jax.ShapeDtypeStruct(w.shape, jnp.bfloat16) for w in to_round],
        scratch_shapes=scratch,
        compiler_params=_cparams(("arbitrary" if to_round else "parallel", "arbitrary" if to_round else "parallel",
                                  "arbitrary")), name=name,
    )(a, b, *[r for r, _ in rows], *to_round)
    return res


def _mm_bias_kernel(a_ref, b_ref, bias_ref, o_ref):
    @pl.when(pl.program_id(2) == 0)
    def _():
        o_ref[...] = jnp.broadcast_to(bias_ref[...], o_ref.shape)

    o_ref[...] += jnp.dot(a_ref[...], b_ref[...], preferred_element_type=jnp.float32)


def _matmul_bias(a, b, bias, *, name, tm, tn, tk):
    M, K = a.shape
    N = b.shape[1]
    tm, tn, tk = min(tm, M), min(tn, N), min(tk, K)
    assert M % tm == 0 and N % tn == 0 and K % tk == 0
    return pl.pallas_call(
        _mm_bias_kernel,
        grid=(M // tm, N // tn, K // tk),
        in_specs=[pl.BlockSpec((tm, tk), lambda i, j, k: (i, k)),
                  pl.BlockSpec((tk, tn), lambda i, j, k: (k, j)),
                  pl.BlockSpec((1, tn), lambda i, j, k: (0, j))],
        out_specs=pl.BlockSpec((tm, tn), lambda i, j, k: (i, j)),
        out_shape=jax.ShapeDtypeStruct((M, N), jnp.float32),
        compiler_params=_cparams(("parallel", "parallel", "arbitrary")), name=name,
    )(a, b, bias)


def _epi_plain(acc, extra, outs):
    outs[0][...] = acc.astype(outs[0].dtype)


def _sigmoid(x):
    return 0.5 * jnp.tanh(0.5 * x) + 0.5


def _epi_gate(acc, extra, outs):
    outs[0][...] = _sigmoid(acc + extra[0][...]).astype(outs[0].dtype)


def _epi_relu2(acc, extra, outs):
    h = jnp.maximum(acc + extra[0][...], 0.0)
    outs[0][...] = (h * h).astype(outs[0].dtype)


def _merge_kernel(at_ref, pa_ref, h_ref, pl_ref, ga_ref, gb_ref, o_ref):
    pa = jnp.dot(at_ref[...], pa_ref[...], preferred_element_type=jnp.float32)
    out = ga_ref[...].astype(jnp.float32) * pa
    pb = jnp.dot(h_ref[...], pl_ref[...], preferred_element_type=jnp.float32)
    o_ref[...] = (out + gb_ref[...].astype(jnp.float32) * pb).astype(o_ref.dtype)


def _merge(attn, p_attn, h, p_lru, gates, *, tm=1024, tn=512):
    M, K1 = attn.shape
    K2 = h.shape[1]
    N = p_attn.shape[1]
    tm, tn = min(tm, M), min(tn, N)
    gb_off = N // tn
    return pl.pallas_call(
        _merge_kernel,
        grid=(M // tm, N // tn),
        in_specs=[
            pl.BlockSpec((tm, K1), lambda i, j: (i, 0)),
            pl.BlockSpec((K1, tn), lambda i, j: (0, j)),
            pl.BlockSpec((tm, K2), lambda i, j: (i, 0)),
            pl.BlockSpec((K2, tn), lambda i, j: (0, j)),
            pl.BlockSpec((tm, tn), lambda i, j: (i, j)),
            pl.BlockSpec((tm, tn), lambda i, j: (i, j + gb_off)),
        ],
        out_specs=pl.BlockSpec((tm, tn), lambda i, j: (i, j)),
        out_shape=jax.ShapeDtypeStruct((M, N), jnp.bfloat16),
        compiler_params=_cparams(("parallel", "arbitrary")), name="merge",
    )(attn, p_attn, h, p_lru, gates, gates)


def _ln_rows(u, g_ref, b_ref, outs):
    mu = jnp.mean(u, axis=-1, keepdims=True)
    c = u - mu
    var = jnp.mean(c * c, axis=-1, keepdims=True)
    y = c * lax.rsqrt(var + LN_EPS) * g_ref[...] + b_ref[...]
    for o in outs:
        o[...] = y.astype(o.dtype)


def _res_ln_kernel(u_ref, res_ref, g_ref, b_ref, *outs, alpha):
    _ln_rows(alpha * res_ref[...] + u_ref[...], g_ref, b_ref, outs)


def _residual_layer_norm(u, res, g, b, out_dtypes, *, alpha, tr=256):
    M, D = u.shape
    tr = min(tr, M)
    rows = pl.BlockSpec((tr, D), lambda i: (i, 0))
    vec = pl.BlockSpec((1, D), lambda i: (0, 0))
    return pl.pallas_call(
        functools.partial(_res_ln_kernel, alpha=alpha),
        grid=(M // tr,),
        in_specs=[rows, rows, vec, vec],
        out_specs=[rows for _ in out_dtypes],
        out_shape=[jax.ShapeDtypeStruct((M, D), dt) for dt in out_dtypes],
        compiler_params=_cparams(("parallel",)), name="layer_norm",
    )(u, res, g, b)


def _proj_ln_kernel(a_ref, w_ref, res_ref, g_ref, b_ref, *outs, alpha):
    u = alpha * res_ref[...] + jnp.dot(a_ref[...], w_ref[...], preferred_element_type=jnp.float32)
    _ln_rows(u, g_ref, b_ref, outs)


def _proj_layer_norm(a, w, res, g, b, out_dtypes, *, alpha, tm=128):
    M, K = a.shape
    N = w.shape[1]
    tm = min(tm, M)
    rows = lambda n: pl.BlockSpec((tm, n), lambda i: (i, 0))
    fixed = lambda shape, **kw: pl.BlockSpec(shape, lambda i: (0, 0), **kw)
    return pl.pallas_call(
        functools.partial(_proj_ln_kernel, alpha=alpha),
        grid=(M // tm,),
        in_specs=[rows(K), fixed((K, N), pipeline_mode=pl.Buffered(1)), rows(N), fixed((1, N)), fixed((1, N))],
        out_specs=[rows(N) for _ in out_dtypes],
        out_shape=[jax.ShapeDtypeStruct((M, N), dt) for dt in out_dtypes],
        compiler_params=_cparams(("parallel",)), name="out_proj_ln",
    )(a, w, res, g, b)


def _band_block(q_ref, k_ref, v_ref, bias_ref, *, base, seq, q0, interior, scale, emit):
    if interior:
        case, kstart = 0, q0 - HALF
    else:
        case = jnp.where(q0 == 0, 1, jnp.where(q0 == seq - QBLK, 2, 0))
        kstart = jnp.clip(q0 - HALF, 0, seq - KWIN)
    q = q_ref[pl.ds(pl.multiple_of(base + q0, QBLK), QBLK), :]
    kw = k_ref[pl.ds(pl.multiple_of(base + kstart, HALF), KWIN), :]
    vw = v_ref[pl.ds(pl.multiple_of(base + kstart, HALF), KWIN), :]
    s = lax.dot_general(q, kw, (((1,), (1,)), ((), ())), preferred_element_type=jnp.float32)
    s = s * (scale * LOG2_E) + bias_ref[case]
    m = jnp.max(s, axis=-1, keepdims=True)
    e = jnp.exp2(s - m)
    den = jnp.sum(e, axis=-1, keepdims=True)
    o = jnp.dot(e.astype(jnp.bfloat16), vw, preferred_element_type=jnp.float32) / den
    emit(o, m * LN_2 + jnp.log(den))


def _attn_kernel(slope_ref, q_ref, k_ref, v_ref, out_ref,
                 qf_ref, kf_ref, vf_ref, qd_ref, kd_ref, vd_ref, o_mid_ref, l_mid_ref, o_far_ref, l_far_ref,
                 bias_ref, *, scale):
    S = q_ref.shape[0]
    slope = slope_ref[pl.program_id(0)]
    col = lax.broadcasted_iota(jnp.int32, (QBLK, KWIN), 1)
    row = lax.broadcasted_iota(jnp.int32, (QBLK, KWIN), 0)
    delta = col - row

    def set_bias(dil):
        for case, off in enumerate((-HALF, 0, -2 * HALF)):
            dist = jnp.abs(delta + off)
            bias_ref[case] = jnp.where(dist <= HALF, dist.astype(jnp.float32) * (-slope * dil * LOG2_E), -jnp.inf)

    n_widen = 2
    piece = S // n_widen

    def widen(i, _):
        rows = pl.ds(pl.multiple_of(i * piece, piece), piece)
        qf_ref[rows, :] = q_ref[rows, :].astype(jnp.float32)
        kf_ref[rows, :] = k_ref[rows, :].astype(jnp.float32)
        vf_ref[rows, :] = v_ref[rows, :].astype(jnp.float32)
        return 0

    lax.fori_loop(0, n_widen, widen, 0)

    for dil, o_ref, l_ref in ((DILATIONS[2], o_far_ref, l_far_ref), (DILATIONS[1], o_mid_ref, l_mid_ref)):
        L = S // dil
        group = dil // ATT_PASSES[dil]
        set_bias(dil)

        def residues(i, _, dil=dil, L=L, group=group, o_ref=o_ref, l_ref=l_ref):
            for g in range(group):
                r = i * group + g
                gather, rows = pl.ds(r, L, stride=dil), pl.ds(pl.multiple_of(r * L, L), L)
                qd_ref[rows, :] = qf_ref[gather, :].astype(jnp.bfloat16)
                kd_ref[rows, :] = kf_ref[gather, :].astype(jnp.bfloat16)
                vd_ref[rows, :] = vf_ref[gather, :].astype(jnp.bfloat16)
                for q0 in range(0, L, QBLK):
                    def emit(o, lse, scatter=pl.ds(q0 * dil + r, QBLK, stride=dil)):
                        o_ref[scatter, :] = o
                        l_ref[scatter, :] = jnp.broadcast_to(lse, (QBLK, HEAD_DIM))

                    _band_block(qd_ref, kd_ref, vd_ref, bias_ref, base=r * L, seq=L, q0=q0,
                                interior=0 < q0 < L - QBLK, scale=scale, emit=emit)
            return 0

        lax.fori_loop(0, ATT_PASSES[dil], residues, 0)

    set_bias(DILATIONS[0])
    n_blk = S // QBLK // ATT_PASSES[1]

    def near(i, _):
        for j in range(n_blk):
            q0 = (i * n_blk + j) * QBLK

            def emit_mixed(o_near, l_near, rows=pl.ds(pl.multiple_of(q0, QBLK), QBLK)):
                l_mid, l_far = l_mid_ref[rows, :], l_far_ref[rows, :]
                m = jnp.maximum(jnp.maximum(l_mid, l_far), l_near)
                w_near, w_mid, w_far = jnp.exp(l_near - m), jnp.exp(l_mid - m), jnp.exp(l_far - m)
                mixed = w_near * o_near + w_mid * o_mid_ref[rows, :] + w_far * o_far_ref[rows, :]
                out_ref[rows, :] = (mixed / (w_near + w_mid + w_far)).astype(out_ref.dtype)

            _band_block(q_ref, k_ref, v_ref, bias_ref, base=0, seq=S, q0=q0, interior=0 < j < n_blk - 1,
                        scale=scale, emit=emit_mixed)
        return 0

    lax.fori_loop(0, ATT_PASSES[1], near, 0)


def _dilated_attention(qkv, slopes):
    S = qkv.shape[0]
    assert DILATIONS[0] == 1 and S % (DILATIONS[2] * KWIN) == 0 and S % (ATT_PASSES[1] * QBLK) == 0
    f32, bf16 = jnp.float32, jnp.bfloat16
    head_col = lambda part: pl.BlockSpec((S, HEAD_DIM), lambda h: (0, part * N_HEADS + h))
    return pl.pallas_call(
        functools.partial(_attn_kernel, scale=HEAD_DIM ** -0.5),
        grid=(N_HEADS,),
        in_specs=[pl.BlockSpec(memory_space=pltpu.SMEM), head_col(0), head_col(1), head_col(2)],
        out_specs=pl.BlockSpec((S, HEAD_DIM), lambda h: (0, h)),
        out_shape=jax.ShapeDtypeStruct((S, ATTN_WIDTH), bf16),
        scratch_shapes=[pltpu.VMEM((S, HEAD_DIM), f32)] * 3
        + [pltpu.VMEM((S, HEAD_DIM), bf16)] * 3
        + [pltpu.VMEM((S, HEAD_DIM), f32)] * 4
        + [pltpu.VMEM((3, QBLK, KWIN), f32)],
        compiler_params=_cparams(("parallel",)), name="dilated_attention",
    )(slopes, qkv, qkv, qkv)


def _scan_local(a_slices, b_slices):
    hs, ps = [], []
    h = p = None
    for a, b in zip(a_slices, b_slices):
        h, p = (b, a) if h is None else (a * h + b, a * p)
        hs.append(h)
        ps.append(p)
    return hs, ps


def _slabs(ref, rows=None):
    parts = [ref[s] if rows is None else ref[s, rows, :] for s in range(ref.shape[0])]
    return jnp.concatenate(parts, axis=1) if len(parts) > 1 else parts[0]


def _to_slabs(ref, value, rows=None):
    for s in range(ref.shape[0]):
        piece = value[:, s * LANES:(s + 1) * LANES]
        if rows is None:
            ref[s] = piece
        else:
            ref[s, rows, :] = piece


def _lru_direction(x_ref, xp_ref, xn_ref, cw_ref, cb_ref, wa_ref, wx_ref, ba_ref, bx_ref, lam_ref,
                   carry_ref, ext_ref, t1a_ref, t1b_ref, e1_ref, g_ref, hs_ref, *, tb, n_tb, reverse):
    T, C = x_ref.shape
    n1 = T // SUBLANES
    n2 = n1 // SUBLANES
    strided = lambda k, n: pl.ds(k, n, stride=SUBLANES)
    _to_slabs(ext_ref, jnp.where(tb > 0, xp_ref[...], 0.0), pl.ds(0, SUBLANES))
    _to_slabs(ext_ref, x_ref[...], pl.ds(SUBLANES, T))
    _to_slabs(ext_ref, jnp.where(tb < n_tb - 1, xn_ref[...], 0.0), pl.ds(SUBLANES + T, SUBLANES))
    tap = {o: _slabs(ext_ref, strided(SUBLANES + o, n1)) for o in range(-CONV_LEFT, SUBLANES + 4 - CONV_LEFT - 1)}
    xk = []
    for k in range(SUBLANES):
        xck = cb_ref[...]
        for j in range(4):
            xck = xck + cw_ref[j:j + 1, :] * tap[k + j - CONV_LEFT]
        xk.append(xck)
    xk = jnp.concatenate(xk, axis=0)
    xkb = xk.astype(jnp.bfloat16)
    ga, gx = [], []
    for n in range(C // LRU_BLOCK):
        blk = xkb[:, n * LRU_BLOCK:(n + 1) * LRU_BLOCK]
        ga.append(jnp.dot(blk, wa_ref[n], preferred_element_type=jnp.float32))
        gx.append(jnp.dot(blk, wx_ref[n], preferred_element_type=jnp.float32))
    ga = jnp.concatenate(ga, axis=1) if len(ga) > 1 else ga[0]
    gx = jnp.concatenate(gx, axis=1) if len(gx) > 1 else gx[0]
    t_r = jnp.tanh(0.5 * ga + 0.5 * ba_ref[...])
    t_i = jnp.tanh(0.5 * gx + 0.5 * bx_ref[...])
    nlam = -lam_ref[...]
    softplus = jnp.maximum(nlam, 0.0) + jnp.log1p(jnp.exp(-jnp.abs(nlam)))
    half_rate = (-0.5 * LRU_C * LOG2_E) * softplus
    a = jnp.exp2(half_rate * t_r + half_rate)
    one_m_a2 = 1.0 - a * a
    root = one_m_a2 * lax.rsqrt(jnp.maximum(one_m_a2, F32_TINY))
    b = root * ((0.5 * t_i + 0.5) * xk)

    order = list(range(SUBLANES))[::-1] if reverse else list(range(SUBLANES))
    hs0, ps0 = _scan_local([a[k * n1:(k + 1) * n1] for k in order], [b[k * n1:(k + 1) * n1] for k in order])
    _to_slabs(t1a_ref, ps0[-1])
    _to_slabs(t1b_ref, hs0[-1])
    hs1, ps1 = _scan_local([_slabs(t1a_ref, strided(k, n2)) for k in order],
                           [_slabs(t1b_ref, strided(k, n2)) for k in order])
    grp_a, grp_b = ps1[-1], hs1[-1]
    carry = carry_ref[...]
    for g in (range(n2 - 1, -1, -1) if reverse else range(n2)):
        g_ref[g:g + 1, :] = carry
        carry = grp_a[g:g + 1, :] * carry + grp_b[g:g + 1, :]
    carry_ref[...] = carry
    enter_grp = g_ref[...]
    for j, k in enumerate(order):
        enter = enter_grp if j == 0 else hs1[j - 1] + ps1[j - 1] * enter_grp
        _to_slabs(e1_ref, enter, strided(k, n2))
    enter_tile = _slabs(e1_ref)
    for j, k in enumerate(order):
        _to_slabs(hs_ref, hs0[j] + ps0[j] * enter_tile, strided(k, n1))
    return _slabs(hs_ref)


def _lru_kernel(xf_ref, xfp_ref, xfn_ref, xb_ref, xbp_ref, xbn_ref, cw_ref, cb_ref,
                waf_ref, wxf_ref, baf_ref, bxf_ref, lamf_ref,
                wab_ref, wxb_ref, bab_ref, bxb_ref, lamb_ref,
                *rest, n_tb, n_round):
    w_in_refs, h_ref, w_out_refs = rest[:n_round], rest[n_round], rest[n_round + 1:2 * n_round + 1]
    carry_f, carry_b, *scan_scratch = rest[2 * n_round + 1:]
    s = pl.program_id(1)
    T = xf_ref.shape[0]
    for w_in_ref, w_out_ref in zip(w_in_refs, w_out_refs):
        w_out_ref[...] = w_in_ref[...].astype(w_out_ref.dtype)

    @pl.when(s == 0)
    def _():
        carry_f[...] = jnp.zeros_like(carry_f)
        carry_b[...] = jnp.zeros_like(carry_b)
        h_ref[...] = jnp.zeros_like(h_ref)

    def emit(rows, h):
        h_ref[rows, :] = (h_ref[rows, :].astype(jnp.float32) + h).astype(h_ref.dtype)

    hf = _lru_direction(xf_ref, xfp_ref, xfn_ref, cw_ref, cb_ref, waf_ref, wxf_ref, baf_ref, bxf_ref,
                        lamf_ref, carry_f, *scan_scratch, tb=s, n_tb=n_tb, reverse=False)
    emit(pl.ds(pl.multiple_of(s * T, T), T), hf)
    sb = n_tb - 1 - s
    hb = _lru_direction(xb_ref, xbp_ref, xbn_ref, cw_ref, cb_ref, wab_ref, wxb_ref, bab_ref, bxb_ref,
                        lamb_ref, carry_b, *scan_scratch, tb=sb, n_tb=n_tb, reverse=True)
    emit(pl.ds(pl.multiple_of(sb * T, T), T), hb)


def _rg_lru(xr, conv_w, conv_b, fwd, bwd, to_round, *, tc=512):
    S, C = xr.shape
    T, tc = LRU_T, min(tc, C)
    assert S % T == 0 and C % tc == 0 and tc % LRU_BLOCK == 0
    n_tb = S // T
    n_steps = (C // tc) * n_tb
    slab_rows = [w.shape[0] // n_steps for w in to_round]
    assert all(w.shape[0] % n_steps == 0 and r % BF16_SUBLANES == 0 for w, r in zip(to_round, slab_rows))
    slab_specs = [pl.BlockSpec((r, w.shape[1]), lambda c, s: (c * n_tb + s, 0)) for w, r in zip(to_round, slab_rows)]
    tpb = T // SUBLANES
    n_halo = S // SUBLANES
    nb = tc // LRU_BLOCK
    n_slab = tc // LANES
    n1 = T // SUBLANES

    def main(tb_of):
        return pl.BlockSpec((T, tc), lambda c, s: (tb_of(s), c))

    def prev(tb_of):
        return pl.BlockSpec((SUBLANES, tc), lambda c, s: (jnp.maximum(tb_of(s) * tpb - 1, 0), c))

    def nxt(tb_of):
        return pl.BlockSpec((SUBLANES, tc), lambda c, s: (jnp.minimum((tb_of(s) + 1) * tpb, n_halo - 1), c))

    fwd_tb = lambda s: s
    bwd_tb = lambda s: n_tb - 1 - s
    row = pl.BlockSpec((1, tc), lambda c, s: (0, c))
    wspec = pl.BlockSpec((nb, LRU_BLOCK, LRU_BLOCK), lambda c, s: (c, 0, 0))
    once = pl.Buffered(1)
    return pl.pallas_call(
        functools.partial(_lru_kernel, n_tb=n_tb, n_round=len(to_round)),
        grid=(C // tc, n_tb),
        in_specs=[main(fwd_tb), prev(fwd_tb), nxt(fwd_tb), main(bwd_tb), prev(bwd_tb), nxt(bwd_tb),
                  pl.BlockSpec((4, tc), lambda c, s: (0, c)), row,
                  wspec, wspec, row, row, row,
                  wspec, wspec, row, row, row] + slab_specs,
        out_specs=[pl.BlockSpec((S, tc), lambda c, s: (0, c), pipeline_mode=once)] + slab_specs,
        out_shape=[jax.ShapeDtypeStruct((S, C), jnp.bfloat16)]
        + [jax.ShapeDtypeStruct(w.shape, jnp.bfloat16) for w in to_round],
        scratch_shapes=[pltpu.VMEM((1, tc), jnp.float32), pltpu.VMEM((1, tc), jnp.float32),
                        pltpu.VMEM((n_slab, T + 2 * SUBLANES, LANES), jnp.float32),
                        pltpu.VMEM((n_slab, n1, LANES), jnp.float32),
                        pltpu.VMEM((n_slab, n1, LANES), jnp.float32),
                        pltpu.VMEM((n_slab, n1, LANES), jnp.float32),
                        pltpu.VMEM((n1 // SUBLANES, tc), jnp.float32),
                        pltpu.VMEM((n_slab, T, LANES), jnp.float32)],
        compiler_params=_cparams(("arbitrary", "arbitrary")), name="rg_lru",
    )(xr, xr, xr, xr, xr, xr, conv_w, conv_b, *fwd, *bwd, *to_round)


def kernel(x, w_in, gate_b, conv_w, conv_b, lru_wa_fwd, lru_ba_fwd, lru_wx_fwd, lru_bx_fwd, lru_lam_fwd, lru_wa_bwd, lru_ba_bwd, lru_wx_bwd, lru_bx_bwd, lru_lam_bwd, p_attn, p_lru, w_out, ln1_g, ln1_b, w_mlp1, b_mlp1, w_mlp2, b_mlp2, ln2_g, ln2_b):
    B, S, D = x.shape
    depth = w_in.shape[0]
    lru_width = conv_w.shape[-1]
    alpha = (2.0 * depth) ** 0.25
    bf16, f32 = jnp.bfloat16, jnp.float32
    slopes = 2.0 ** (-8.0 * jnp.arange(1, N_HEADS + 1, dtype=f32) / N_HEADS)
    row = lambda v: v.reshape(1, -1).astype(f32)

    outs = []
    for bi in range(B):
        xs = x[bi]
        for l in range(depth):
            xb = xs.astype(bf16)
            w_in_l = w_in[l]
            qkv, = _matmul(xb, w_in_l, name="proj_qkv", n_cols=3 * ATTN_WIDTH, col_off=0, epilogue=_epi_plain,
                           out_dtypes=[bf16], tn=F32_WEIGHT_TN)
            xr, = _matmul(xb, w_in_l, name="proj_lru", n_cols=lru_width, col_off=3 * ATTN_WIDTH,
                          epilogue=_epi_plain, out_dtypes=[f32], tn=F32_WEIGHT_TN)
            gates, = _matmul(xb, w_in_l, name="proj_gates", n_cols=2 * D, col_off=3 * ATTN_WIDTH + lru_width,
                             epilogue=_epi_gate, out_dtypes=[bf16], rows=[(row(gate_b[l]), 0)], tn=F32_WEIGHT_TN)

            attn = _dilated_attention(qkv, slopes)

            lru = lambda wa, ba, wx, bx, lam: (wa[l].astype(bf16), wx[l].astype(bf16), row(ba[l]), row(bx[l]),
                                               row(lam[l]))
            h, p_attn_b, p_lru_b, w_out_b, w_mlp1_b = _rg_lru(
                xr, conv_w[l], row(conv_b[l]),
                lru(lru_wa_fwd, lru_ba_fwd, lru_wx_fwd, lru_bx_fwd, lru_lam_fwd),
                lru(lru_wa_bwd, lru_ba_bwd, lru_wx_bwd, lru_bx_bwd, lru_lam_bwd),
                [p_attn[l], p_lru[l], w_out[l], w_mlp1[l]])

            merged = _merge(attn, p_attn_b, h, p_lru_b, gates)
            y, yb = _proj_layer_norm(merged, w_out_b, xs, row(ln1_g[l]), row(ln1_b[l]), [f32, bf16], alpha=alpha)

            hid, w_mlp2_b = _matmul(yb, w_mlp1_b, name="mlp1", n_cols=w_mlp1.shape[-1], col_off=0,
                                    epilogue=_epi_relu2, out_dtypes=[bf16], rows=[(row(b_mlp1[l]), 0)],
                                    to_round=[w_mlp2[l]])
            u2 = _matmul_bias(hid, w_mlp2_b, row(b_mlp2[l]), name="mlp2", tm=1024, tn=1024, tk=4096)
            xs, = _residual_layer_norm(u2, y, row(ln2_g[l]), row(ln2_b[l]), [f32], alpha=alpha)
        outs.append(xs)
    return outs[0][None] if B == 1 else jnp.stack(outs, axis=0)
```

```python
import functools

import jax
import jax.numpy as jnp
from jax import lax
from jax.experimental import pallas as pl
from jax.experimental.pallas import tpu as pltpu

HEAD_DIM = 128
N_HEADS = 16
ATTN_WIDTH = N_HEADS * HEAD_DIM
DILATIONS = (1, 4, 16)
HALF = 64
LRU_BLOCK = 256
LRU_C = 8.0
LN_EPS = 1e-5
CONV_LEFT = 2

V7X_VMEM_LIMIT_BYTES = 56 * 1024 * 1024
SUBLANES = 8
BF16_SUBLANES = 16
LANES = 128

LRU_T = 512
LOG2_E = 1.4426950408889634
LN_2 = 0.6931471805599453
F32_TINY = 1e-37
QBLK = 128
KWIN = QBLK + 2 * HALF
ATT_PASSES = {1: 2, 4: 1, 16: 4}
F32_WEIGHT_TN = 512


def _cparams(sem):
    return pltpu.CompilerParams(dimension_semantics=sem, vmem_limit_bytes=V7X_VMEM_LIMIT_BYTES)


def _mm_kernel(*refs, nk, n_extra, n_out, n_round, epilogue):
    a_ref, b_ref = refs[0], refs[1]
    extra = refs[2:2 + n_extra]
    outs = refs[2 + n_extra + n_round:2 + n_extra + n_round + n_out]
    for src, dst in zip(refs[2 + n_extra:2 + n_extra + n_round], refs[2 + n_extra + n_round + n_out:]):
        dst[...] = src[...].astype(dst.dtype)
    prod = jnp.dot(a_ref[...], b_ref[...].astype(jnp.bfloat16), preferred_element_type=jnp.float32)
    if nk == 1:
        epilogue(prod, extra, outs)
        return
    acc_ref = refs[-1]
    k = pl.program_id(2)

    @pl.when(k == 0)
    def _():
        acc_ref[...] = jnp.zeros_like(acc_ref)

    acc_ref[...] += prod

    @pl.when(k == nk - 1)
    def _():
        epilogue(acc_ref[...], extra, outs)


def _matmul(a, b, *, name, n_cols, col_off, epilogue, out_dtypes, rows=(), to_round=(), tm=1024, tn=1024,
            tk=4096):
    M, K = a.shape
    tm, tn, tk = min(tm, M), min(tn, n_cols), min(tk, K)
    while col_off % tn or any(r_off % tn for _, r_off in rows):
        tn //= 2
    assert M % tm == 0 and n_cols % tn == 0 and K % tk == 0 and tn % LANES == 0
    nk = K // tk
    nj = n_cols // tn
    off = col_off // tn
    in_specs = [pl.BlockSpec((tm, tk), lambda i, j, k: (i, k)),
                pl.BlockSpec((tk, tn), lambda i, j, k: (k, j + off))]
    for r, r_off in rows:
        ro = r_off // tn
        in_specs.append(pl.BlockSpec((1, tn), lambda i, j, k, ro=ro: (0, j + ro)))
    n_steps = (M // tm) * nj
    assert not to_round or nk == 1
    assert all(w.shape[0] % (n_steps * BF16_SUBLANES) == 0 for w in to_round)
    slab_specs = [pl.BlockSpec((w.shape[0] // n_steps, w.shape[1]), lambda i, j, k: (i * nj + j, 0))
                  for w in to_round]
    out_specs = [pl.BlockSpec((tm, tn), lambda i, j, k: (i, j)) for _ in out_dtypes]
    out_shape = [jax.ShapeDtypeStruct((M, n_cols), dt) for dt in out_dtypes]
    scratch = [pltpu.VMEM((tm, tn), jnp.float32)] if nk > 1 else []
    res = pl.pallas_call(
        functools.partial(_mm_kernel, nk=nk, n_extra=len(rows), n_out=len(out_dtypes), n_round=len(to_round),
                          epilogue=epilogue),
        grid=(M // tm, nj, nk),
        in_specs=in_specs + slab_specs, out_specs=out_specs + slab_specs,
        out_shape=out_shape + [jax.ShapeDtypeStruct(w.shape, jnp.bfloat16) for w in to_round],
        scratch_shapes=scratch,
        compiler_params=_cparams(("arbitrary" if to_round else "parallel", "arbitrary" if to_round else "parallel",
                                  "arbitrary")), name=name,
    )(a, b, *[r for r, _ in rows], *to_round)
    return res


def _mm_bias_kernel(a_ref, b_ref, bias_ref, o_ref):
    @pl.when(pl.program_id(2) == 0)
    def _():
        o_ref[...] = jnp.broadcast_to(bias_ref[...], o_ref.shape)

    o_ref[...] += jnp.dot(a_ref[...], b_ref[...], preferred_element_type=jnp.float32)


def _matmul_bias(a, b, bias, *, name, tm, tn, tk):
    M, K = a.shape
    N = b.shape[1]
    tm, tn, tk = min(tm, M), min(tn, N), min(tk, K)
    assert M % tm == 0 and N % tn == 0 and K % tk == 0
    return pl.pallas_call(
        _mm_bias_kernel,
        grid=(M // tm, N // tn, K // tk),
        in_specs=[pl.BlockSpec((tm, tk), lambda i, j, k: (i, k)),
                  pl.BlockSpec((tk, tn), lambda i, j, k: (k, j)),
                  pl.BlockSpec((1, tn), lambda i, j, k: (0, j))],
        out_specs=pl.BlockSpec((tm, tn), lambda i, j, k: (i, j)),
        out_shape=jax.ShapeDtypeStruct((M, N), jnp.float32),
        compiler_params=_cparams(("parallel", "parallel", "arbitrary")), name=name,
    )(a, b, bias)


def _epi_plain(acc, extra, outs):
    outs[0][...] = acc.astype(outs[0].dtype)


def _sigmoid(x):
    return 0.5 * jnp.tanh(0.5 * x) + 0.5


def _epi_gate(acc, extra, outs):
    outs[0][...] = _sigmoid(acc + extra[0][...]).astype(outs[0].dtype)


def _epi_relu2(acc, extra, outs):
    h = jnp.maximum(acc + extra[0][...], 0.0)
    outs[0][...] = (h * h).astype(outs[0].dtype)


def _merge_kernel(at_ref, pa_ref, h_ref, pl_ref, ga_ref, gb_ref, o_ref):
    pa = jnp.dot(at_ref[...], pa_ref[...], preferred_element_type=jnp.float32)
    out = ga_ref[...].astype(jnp.float32) * pa
    pb = jnp.dot(h_ref[...], pl_ref[...], preferred_element_type=jnp.float32)
    o_ref[...] = (out + gb_ref[...].astype(jnp.float32) * pb).astype(o_ref.dtype)


def _merge(attn, p_attn, h, p_lru, gates, *, tm=1024, tn=512):
    M, K1 = attn.shape
    K2 = h.shape[1]
    N = p_attn.shape[1]
    tm, tn = min(tm, M), min(tn, N)
    gb_off = N // tn
    return pl.pallas_call(
        _merge_kernel,
        grid=(M // tm, N // tn),
        in_specs=[
            pl.BlockSpec((tm, K1), lambda i, j: (i, 0)),
            pl.BlockSpec((K1, tn), lambda i, j: (0, j)),
            pl.BlockSpec((tm, K2), lambda i, j: (i, 0)),
            pl.BlockSpec((K2, tn), lambda i, j: (0, j)),
            pl.BlockSpec((tm, tn), lambda i, j: (i, j)),
            pl.BlockSpec((tm, tn), lambda i, j: (i, j + gb_off)),
        ],
        out_specs=pl.BlockSpec((tm, tn), lambda i, j: (i, j)),
        out_shape=jax.ShapeDtypeStruct((M, N), jnp.bfloat16),
        compiler_params=_cparams(("parallel", "arbitrary")), name="merge",
    )(attn, p_attn, h, p_lru, gates, gates)


def _ln_rows(u, g_ref, b_ref, outs):
    mu = jnp.mean(u, axis=-1, keepdims=True)
    c = u - mu
    var = jnp.mean(c * c, axis=-1, keepdims=True)
    y = c * lax.rsqrt(var + LN_EPS) * g_ref[...] + b_ref[...]
    for o in outs:
        o[...] = y.astype(o.dtype)


def _res_ln_kernel(u_ref, res_ref, g_ref, b_ref, *outs, alpha):
    _ln_rows(alpha * res_ref[...] + u_ref[...], g_ref, b_ref, outs)


def _residual_layer_norm(u, res, g, b, out_dtypes, *, alpha, tr=256):
    M, D = u.shape
    tr = min(tr, M)
    rows = pl.BlockSpec((tr, D), lambda i: (i, 0))
    vec = pl.BlockSpec((1, D), lambda i: (0, 0))
    return pl.pallas_call(
        functools.partial(_res_ln_kernel, alpha=alpha),
        grid=(M // tr,),
        in_specs=[rows, rows, vec, vec],
        out_specs=[rows for _ in out_dtypes],
        out_shape=[jax.ShapeDtypeStruct((M, D), dt) for dt in out_dtypes],
        compiler_params=_cparams(("parallel",)), name="layer_norm",
    )(u, res, g, b)


def _proj_ln_kernel(a_ref, w_ref, res_ref, g_ref, b_ref, *outs, alpha):
    u = alpha * res_ref[...] + jnp.dot(a_ref[...], w_ref[...], preferred_element_type=jnp.float32)
    _ln_rows(u, g_ref, b_ref, outs)


def _proj_layer_norm(a, w, res, g, b, out_dtypes, *, alpha, tm=128):
    M, K = a.shape
    N = w.shape[1]
    tm = min(tm, M)
    rows = lambda n: pl.BlockSpec((tm, n), lambda i: (i, 0))
    fixed = lambda shape, **kw: pl.BlockSpec(shape, lambda i: (0, 0), **kw)
    return pl.pallas_call(
        functools.partial(_proj_ln_kernel, alpha=alpha),
        grid=(M // tm,),
        in_specs=[rows(K), fixed((K, N), pipeline_mode=pl.Buffered(1)), rows(N), fixed((1, N)), fixed((1, N))],
        out_specs=[rows(N) for _ in out_dtypes],
        out_shape=[jax.ShapeDtypeStruct((M, N), dt) for dt in out_dtypes],
        compiler_params=_cparams(("parallel",)), name="out_proj_ln",
    )(a, w, res, g, b)


def _band_block(q_ref, k_ref, v_ref, bias_ref, *, base, seq, q0, interior, scale, emit):
    if interior:
        case, kstart = 0, q0 - HALF
    else:
        case = jnp.where(q0 == 0, 1, jnp.where(q0 == seq - QBLK, 2, 0))
        kstart = jnp.clip(q0 - HALF, 0, seq - KWIN)
    q = q_ref[pl.ds(pl.multiple_of(base + q0, QBLK), QBLK), :]
    kw = k_ref[pl.ds(pl.multiple_of(base + kstart, HALF), KWIN), :]
    vw = v_ref[pl.ds(pl.multiple_of(base + kstart, HALF), KWIN), :]
    s = lax.dot_general(q, kw, (((1,), (1,)), ((), ())), preferred_element_type=jnp.float32)
    s = s * (scale * LOG2_E) + bias_ref[case]
    m = jnp.max(s, axis=-1, keepdims=True)
    e = jnp.exp2(s - m)
    den = jnp.sum(e, axis=-1, keepdims=True)
    o = jnp.dot(e.astype(jnp.bfloat16), vw, preferred_element_type=jnp.float32) / den
    emit(o, m * LN_2 + jnp.log(den))


def _attn_kernel(slope_ref, q_ref, k_ref, v_ref, out_ref,
                 qf_ref, kf_ref, vf_ref, qd_ref, kd_ref, vd_ref, o_mid_ref, l_mid_ref, o_far_ref, l_far_ref,
                 bias_ref, *, scale):
    S = q_ref.shape[0]
    slope = slope_ref[pl.program_id(0)]
    col = lax.broadcasted_iota(jnp.int32, (QBLK, KWIN), 1)
    row = lax.broadcasted_iota(jnp.int32, (QBLK, KWIN), 0)
    delta = col - row

    def set_bias(dil):
        for case, off in enumerate((-HALF, 0, -2 * HALF)):
            dist = jnp.abs(delta + off)
            bias_ref[case] = jnp.where(dist <= HALF, dist.astype(jnp.float32) * (-slope * dil * LOG2_E), -jnp.inf)

    n_widen = 2
    piece = S // n_widen

    def widen(i, _):
        rows = pl.ds(pl.multiple_of(i * piece, piece), piece)
        qf_ref[rows, :] = q_ref[rows, :].astype(jnp.float32)
        kf_ref[rows, :] = k_ref[rows, :].astype(jnp.float32)
        vf_ref[rows, :] = v_ref[rows, :].astype(jnp.float32)
        return 0

    lax.fori_loop(0, n_widen, widen, 0)

    for dil, o_ref, l_ref in ((DILATIONS[2], o_far_ref, l_far_ref), (DILATIONS[1], o_mid_ref, l_mid_ref)):
        L = S // dil
        group = dil // ATT_PASSES[dil]
        set_bias(dil)

        def residues(i, _, dil=dil, L=L, group=group, o_ref=o_ref, l_ref=l_ref):
            for g in range(group):
                r = i * group + g
                gather, rows = pl.ds(r, L, stride=dil), pl.ds(pl.multiple_of(r * L, L), L)
                qd_ref[rows, :] = qf_ref[gather, :].astype(jnp.bfloat16)
                kd_ref[rows, :] = kf_ref[gather, :].astype(jnp.bfloat16)
                vd_ref[rows, :] = vf_ref[gather, :].astype(jnp.bfloat16)
                for q0 in range(0, L, QBLK):
                    def emit(o, lse, scatter=pl.ds(q0 * dil + r, QBLK, stride=dil)):
                        o_ref[scatter, :] = o
                        l_ref[scatter, :] = jnp.broadcast_to(lse, (QBLK, HEAD_DIM))

                    _band_block(qd_ref, kd_ref, vd_ref, bias_ref, base=r * L, seq=L, q0=q0,
                                interior=0 < q0 < L - QBLK, scale=scale, emit=emit)
            return 0

        lax.fori_loop(0, ATT_PASSES[dil], residues, 0)

    set_bias(DILATIONS[0])
    n_blk = S // QBLK // ATT_PASSES[1]

    def near(i, _):
        for j in range(n_blk):
            q0 = (i * n_blk + j) * QBLK

            def emit_mixed(o_near, l_near, rows=pl.ds(pl.multiple_of(q0, QBLK), QBLK)):
                l_mid, l_far = l_mid_ref[rows, :], l_far_ref[rows, :]
                m = jnp.maximum(jnp.maximum(l_mid, l_far), l_near)
                w_near, w_mid, w_far = jnp.exp(l_near - m), jnp.exp(l_mid - m), jnp.exp(l_far - m)
                mixed = w_near * o_near + w_mid * o_mid_ref[rows, :] + w_far * o_far_ref[rows, :]
                out_ref[rows, :] = (mixed / (w_near + w_mid + w_far)).astype(out_ref.dtype)

            _band_block(q_ref, k_ref, v_ref, bias_ref, base=0, seq=S, q0=q0, interior=0 < j < n_blk - 1,
                        scale=scale, emit=emit_mixed)
        return 0

    lax.fori_loop(0, ATT_PASSES[1], near, 0)


def _dilated_attention(qkv, slopes):
    S = qkv.shape[0]
    assert DILATIONS[0] == 1 and S % (DILATIONS[2] * KWIN) == 0 and S % (ATT_PASSES[1] * QBLK) == 0
    f32, bf16 = jnp.float32, jnp.bfloat16
    head_col = lambda part: pl.BlockSpec((S, HEAD_DIM), lambda h: (0, part * N_HEADS + h))
    return pl.pallas_call(
        functools.partial(_attn_kernel, scale=HEAD_DIM ** -0.5),
        grid=(N_HEADS,),
        in_specs=[pl.BlockSpec(memory_space=pltpu.SMEM), head_col(0), head_col(1), head_col(2)],
        out_specs=pl.BlockSpec((S, HEAD_DIM), lambda h: (0, h)),
        out_shape=jax.ShapeDtypeStruct((S, ATTN_WIDTH), bf16),
        scratch_shapes=[pltpu.VMEM((S, HEAD_DIM), f32)] * 3
        + [pltpu.VMEM((S, HEAD_DIM), bf16)] * 3
        + [pltpu.VMEM((S, HEAD_DIM), f32)] * 4
        + [pltpu.VMEM((3, QBLK, KWIN), f32)],
        compiler_params=_cparams(("parallel",)), name="dilated_attention",
    )(slopes, qkv, qkv, qkv)


def _scan_local(a_slices, b_slices):
    hs, ps = [], []
    h = p = None
    for a, b in zip(a_slices, b_slices):
        h, p = (b, a) if h is None else (a * h + b, a * p)
        hs.append(h)
        ps.append(p)
    return hs, ps


def _slabs(ref, rows=None):
    parts = [ref[s] if rows is None else ref[s, rows, :] for s in range(ref.shape[0])]
    return jnp.concatenate(parts, axis=1) if len(parts) > 1 else parts[0]


def _to_slabs(ref, value, rows=None):
    for s in range(ref.shape[0]):
        piece = value[:, s * LANES:(s + 1) * LANES]
        if rows is None:
            ref[s] = piece
        else:
            ref[s, rows, :] = piece


def _lru_direction(x_ref, xp_ref, xn_ref, cw_ref, cb_ref, wa_ref, wx_ref, ba_ref, bx_ref, lam_ref,
                   carry_ref, ext_ref, t1a_ref, t1b_ref, e1_ref, g_ref, hs_ref, *, tb, n_tb, reverse):
    T, C = x_ref.shape
    n1 = T // SUBLANES
    n2 = n1 // SUBLANES
    strided = lambda k, n: pl.ds(k, n, stride=SUBLANES)
    _to_slabs(ext_ref, jnp.where(tb > 0, xp_ref[...], 0.0), pl.ds(0, SUBLANES))
    _to_slabs(ext_ref, x_ref[...], pl.ds(SUBLANES, T))
    _to_slabs(ext_ref, jnp.where(tb < n_tb - 1, xn_ref[...], 0.0), pl.ds(SUBLANES + T, SUBLANES))
    tap = {o: _slabs(ext_ref, strided(SUBLANES + o, n1)) for o in range(-CONV_LEFT, SUBLANES + 4 - CONV_LEFT - 1)}
    xk = []
    for k in range(SUBLANES):
        xck = cb_ref[...]
        for j in range(4):
            xck = xck + cw_ref[j:j + 1, :] * tap[k + j - CONV_LEFT]
        xk.append(xck)
    xk = jnp.concatenate(xk, axis=0)
    xkb = xk.astype(jnp.bfloat16)
    ga, gx = [], []
    for n in range(C // LRU_BLOCK):
        blk = xkb[:, n * LRU_BLOCK:(n + 1) * LRU_BLOCK]
        ga.append(jnp.dot(blk, wa_ref[n], preferred_element_type=jnp.float32))
        gx.append(jnp.dot(blk, wx_ref[n], preferred_element_type=jnp.float32))
    ga = jnp.concatenate(ga, axis=1) if len(ga) > 1 else ga[0]
    gx = jnp.concatenate(gx, axis=1) if len(gx) > 1 else gx[0]
    t_r = jnp.tanh(0.5 * ga + 0.5 * ba_ref[...])
    t_i = jnp.tanh(0.5 * gx + 0.5 * bx_ref[...])
    nlam = -lam_ref[...]
    softplus = jnp.maximum(nlam, 0.0) + jnp.log1p(jnp.exp(-jnp.abs(nlam)))
    half_rate = (-0.5 * LRU_C * LOG2_E) * softplus
    a = jnp.exp2(half_rate * t_r + half_rate)
    one_m_a2 = 1.0 - a * a
    root = one_m_a2 * lax.rsqrt(jnp.maximum(one_m_a2, F32_TINY))
    b = root * ((0.5 * t_i + 0.5) * xk)

    order = list(range(SUBLANES))[::-1] if reverse else list(range(SUBLANES))
    hs0, ps0 = _scan_local([a[k * n1:(k + 1) * n1] for k in order], [b[k * n1:(k + 1) * n1] for k in order])
    _to_slabs(t1a_ref, ps0[-1])
    _to_slabs(t1b_ref, hs0[-1])
    hs1, ps1 = _scan_local([_slabs(t1a_ref, strided(k, n2)) for k in order],
                           [_slabs(t1b_ref, strided(k, n2)) for k in order])
    grp_a, grp_b = ps1[-1], hs1[-1]
    carry = carry_ref[...]
    for g in (range(n2 - 1, -1, -1) if reverse else range(n2)):
        g_ref[g:g + 1, :] = carry
        carry = grp_a[g:g + 1, :] * carry + grp_b[g:g + 1, :]
    carry_ref[...] = carry
    enter_grp = g_ref[...]
    for j, k in enumerate(order):
        enter = enter_grp if j == 0 else hs1[j - 1] + ps1[j - 1] * enter_grp
        _to_slabs(e1_ref, enter, strided(k, n2))
    enter_tile = _slabs(e1_ref)
    for j, k in enumerate(order):
        _to_slabs(hs_ref, hs0[j] + ps0[j] * enter_tile, strided(k, n1))
    return _slabs(hs_ref)


def _lru_kernel(xf_ref, xfp_ref, xfn_ref, xb_ref, xbp_ref, xbn_ref, cw_ref, cb_ref,
                waf_ref, wxf_ref, baf_ref, bxf_ref, lamf_ref,
                wab_ref, wxb_ref, bab_ref, bxb_ref, lamb_ref,
                *rest, n_tb, n_round):
    w_in_refs, h_ref, w_out_refs = rest[:n_round], rest[n_round], rest[n_round + 1:2 * n_round + 1]
    carry_f, carry_b, *scan_scratch = rest[2 * n_round + 1:]
    s = pl.program_id(1)
    T = xf_ref.shape[0]
    for w_in_ref, w_out_ref in zip(w_in_refs, w_out_refs):
        w_out_ref[...] = w_in_ref[...].astype(w_out_ref.dtype)

    @pl.when(s == 0)
    def _():
        carry_f[...] = jnp.zeros_like(carry_f)
        carry_b[...] = jnp.zeros_like(carry_b)
        h_ref[...] = jnp.zeros_like(h_ref)

    def emit(rows, h):
        h_ref[rows, :] = (h_ref[rows, :].astype(jnp.float32) + h).astype(h_ref.dtype)

    hf = _lru_direction(xf_ref, xfp_ref, xfn_ref, cw_ref, cb_ref, waf_ref, wxf_ref, baf_ref, bxf_ref,
                        lamf_ref, carry_f, *scan_scratch, tb=s, n_tb=n_tb, reverse=False)
    emit(pl.ds(pl.multiple_of(s * T, T), T), hf)
    sb = n_tb - 1 - s
    hb = _lru_direction(xb_ref, xbp_ref, xbn_ref, cw_ref, cb_ref, wab_ref, wxb_ref, bab_ref, bxb_ref,
                        lamb_ref, carry_b, *scan_scratch, tb=sb, n_tb=n_tb, reverse=True)
    emit(pl.ds(pl.multiple_of(sb * T, T), T), hb)


def _rg_lru(xr, conv_w, conv_b, fwd, bwd, to_round, *, tc=512):
    S, C = xr.shape
    T, tc = LRU_T, min(tc, C)
    assert S % T == 0 and C % tc == 0 and tc % LRU_BLOCK == 0
    n_tb = S // T
    n_steps = (C // tc) * n_tb
    slab_rows = [w.shape[0] // n_steps for w in to_round]
    assert all(w.shape[0] % n_steps == 0 and r % BF16_SUBLANES == 0 for w, r in zip(to_round, slab_rows))
    slab_specs = [pl.BlockSpec((r, w.shape[1]), lambda c, s: (c * n_tb + s, 0)) for w, r in zip(to_round, slab_rows)]
    tpb = T // SUBLANES
    n_halo = S // SUBLANES
    nb = tc // LRU_BLOCK
    n_slab = tc // LANES
    n1 = T // SUBLANES

    def main(tb_of):
        return pl.BlockSpec((T, tc), lambda c, s: (tb_of(s), c))

    def prev(tb_of):
        return pl.BlockSpec((SUBLANES, tc), lambda c, s: (jnp.maximum(tb_of(s) * tpb - 1, 0), c))

    def nxt(tb_of):
        return pl.BlockSpec((SUBLANES, tc), lambda c, s: (jnp.minimum((tb_of(s) + 1) * tpb, n_halo - 1), c))

    fwd_tb = lambda s: s
    bwd_tb = lambda s: n_tb - 1 - s
    row = pl.BlockSpec((1, tc), lambda c, s: (0, c))
    wspec = pl.BlockSpec((nb, LRU_BLOCK, LRU_BLOCK), lambda c, s: (c, 0, 0))
    once = pl.Buffered(1)
    return pl.pallas_call(
        functools.partial(_lru_kernel, n_tb=n_tb, n_round=len(to_round)),
        grid=(C // tc, n_tb),
        in_specs=[main(fwd_tb), prev(fwd_tb), nxt(fwd_tb), main(bwd_tb), prev(bwd_tb), nxt(bwd_tb),
                  pl.BlockSpec((4, tc), lambda c, s: (0, c)), row,
                  wspec, wspec, row, row, row,
                  wspec, wspec, row, row, row] + slab_specs,
        out_specs=[pl.BlockSpec((S, tc), lambda c, s: (0, c), pipeline_mode=once)] + slab_specs,
        out_shape=[jax.ShapeDtypeStruct((S, C), jnp.bfloat16)]
        + [jax.ShapeDtypeStruct(w.shape, jnp.bfloat16) for w in to_round],
        scratch_shapes=[pltpu.VMEM((1, tc), jnp.float32), pltpu.VMEM((1, tc), jnp.float32),
                        pltpu.VMEM((n_slab, T + 2 * SUBLANES, LANES), jnp.float32),
                        pltpu.VMEM((n_slab, n1, LANES), jnp.float32),
                        pltpu.VMEM((n_slab, n1, LANES), jnp.float32),
                        pltpu.VMEM((n_slab, n1, LANES), jnp.float32),
                        pltpu.VMEM((n1 // SUBLANES, tc), jnp.float32),
                        pltpu.VMEM((n_slab, T, LANES), jnp.float32)],
        compiler_params=_cparams(("arbitrary", "arbitrary")), name="rg_lru",
    )(xr, xr, xr, xr, xr, xr, conv_w, conv_b, *fwd, *bwd, *to_round)


def kernel(x, w_in, gate_b, conv_w, conv_b, lru_wa_fwd, lru_ba_fwd, lru_wx_fwd, lru_bx_fwd, lru_lam_fwd, lru_wa_bwd, lru_ba_bwd, lru_wx_bwd, lru_bx_bwd, lru_lam_bwd, p_attn, p_lru, w_out, ln1_g, ln1_b, w_mlp1, b_mlp1, w_mlp2, b_mlp2, ln2_g, ln2_b):
    B, S, D = x.shape
    depth = w_in.shape[0]
    lru_width = conv_w.shape[-1]
    alpha = (2.0 * depth) ** 0.25
    bf16, f32 = jnp.bfloat16, jnp.float32
    slopes = 2.0 ** (-8.0 * jnp.arange(1, N_HEADS + 1, dtype=f32) / N_HEADS)
    row = lambda v: v.reshape(1, -1).astype(f32)

    outs = []
    for bi in range(B):
        xs = x[bi]
        for l in range(depth):
            xb = xs.astype(bf16)
            xr, = _matmul(xb, w_in[l], name="proj_lru", n_cols=lru_width, col_off=3 * ATTN_WIDTH,
                          epilogue=_epi_plain, out_dtypes=[f32], tn=F32_WEIGHT_TN)

            lru = lambda wa, ba, wx, bx, lam: (wa[l].astype(bf16), wx[l].astype(bf16), row(ba[l]), row(bx[l]),
                                               row(lam[l]))
            h, p_attn_b, p_lru_b, w_out_b, w_in_b = _rg_lru(
                xr, conv_w[l], row(conv_b[l]),
                lru(lru_wa_fwd, lru_ba_fwd, lru_wx_fwd, lru_bx_fwd, lru_lam_fwd),
                lru(lru_wa_bwd, lru_ba_bwd, lru_wx_bwd, lru_bx_bwd, lru_lam_bwd),
                [p_attn[l], p_lru[l], w_out[l], w_in[l]])

            qkv, = _matmul(xb, w_in_b, name="proj_qkv", n_cols=3 * ATTN_WIDTH, col_off=0, epilogue=_epi_plain,
                           out_dtypes=[bf16])
            attn = _dilated_attention(qkv, slopes)
            gates, w_mlp1_b = _matmul(xb, w_in_b, name="proj_gates", n_cols=2 * D, col_off=3 * ATTN_WIDTH + lru_width,
                                      epilogue=_epi_gate, out_dtypes=[bf16], rows=[(row(gate_b[l]), 0)],
                                      to_round=[w_mlp1[l]], tn=512)

            merged = _merge(attn, p_attn_b, h, p_lru_b, gates)
            y, yb = _proj_layer_norm(merged, w_out_b, xs, row(ln1_g[l]), row(ln1_b[l]), [f32, bf16], alpha=alpha)

            hid, w_mlp2_b = _matmul(yb, w_mlp1_b, name="mlp1", n_cols=w_mlp1.shape[-1], col_off=0,
                                    epilogue=_epi_relu2, out_dtypes=[bf16], rows=[(row(b_mlp1[l]), 0)],
                                    to_round=[w_mlp2[l]])
            u2 = _matmul_bias(hid, w_mlp2_b, row(b_mlp2[l]), name="mlp2", tm=1024, tn=1024, tk=4096)
            xs, = _residual_layer_norm(u2, y, row(ln2_g[l]), row(ln2_b[l]), [f32], alpha=alpha)
        outs.append(xs)
    return outs[0][None] if B == 1 else jnp.stack(outs, axis=0)
```

```python
import functools

import jax
import jax.numpy as jnp
from jax import lax
from jax.experimental import pallas as pl
from jax.experimental.pallas import tpu as pltpu

HEAD_DIM = 128
N_HEADS = 16
ATTN_WIDTH = N_HEADS * HEAD_DIM
DILATIONS = (1, 4, 16)
HALF = 64
LRU_BLOCK = 256
LRU_C = 8.0
LN_EPS = 1e-5
CONV_LEFT = 2

V7X_VMEM_LIMIT_BYTES = 56 * 1024 * 1024
SUBLANES = 8
BF16_SUBLANES = 16
LANES = 128

LRU_T = 512
LOG2_E = 1.4426950408889634
LN_2 = 0.6931471805599453
F32_TINY = 1e-37
QBLK = 128
KWIN = QBLK + 2 * HALF
ATT_PASSES = {1: 2, 4: 1, 16: 4}
F32_WEIGHT_TN = 512


def _cparams(sem):
    return pltpu.CompilerParams(dimension_semantics=sem, vmem_limit_bytes=V7X_VMEM_LIMIT_BYTES)


def _mm_kernel(*refs, nk, n_extra, n_out, n_round, epilogue):
    a_ref, b_ref = refs[0], refs[1]
    extra = refs[2:2 + n_extra]
    outs = refs[2 + n_extra + n_round:2 + n_extra + n_round + n_out]
    for src, dst in zip(refs[2 + n_extra:2 + n_extra + n_round], refs[2 + n_extra + n_round + n_out:]):
        dst[...] = src[...].astype(dst.dtype)
    prod = jnp.dot(a_ref[...], b_ref[...].astype(jnp.bfloat16), preferred_element_type=jnp.float32)
    if nk == 1:
        epilogue(prod, extra, outs)
        return
    acc_ref = refs[-1]
    k = pl.program_id(2)

    @pl.when(k == 0)
    def _():
        acc_ref[...] = jnp.zeros_like(acc_ref)

    acc_ref[...] += prod

    @pl.when(k == nk - 1)
    def _():
        epilogue(acc_ref[...], extra, outs)


def _matmul(a, b, *, name, n_cols, col_off, epilogue, out_dtypes, rows=(), to_round=(), tm=1024, tn=1024,
            tk=4096):
    M, K = a.shape
    tm, tn, tk = min(tm, M), min(tn, n_cols), min(tk, K)
    while col_off % tn or any(r_off % tn for _, r_off in rows):
        tn //= 2
    assert M % tm == 0 and n_cols % tn == 0 and K % tk == 0 and tn % LANES == 0
    nk = K // tk
    nj = n_cols // tn
    off = col_off // tn
    in_specs = [pl.BlockSpec((tm, tk), lambda i, j, k: (i, k)),
                pl.BlockSpec((tk, tn), lambda i, j, k: (k, j + off))]
    for r, r_off in rows:
        ro = r_off // tn
        in_specs.append(pl.BlockSpec((1, tn), lambda i, j, k, ro=ro: (0, j + ro)))
    n_steps = (M // tm) * nj
    assert not to_round or nk == 1
    assert all(w.shape[0] % (n_steps * BF16_SUBLANES) == 0 for w in to_round)
    slab_specs = [pl.BlockSpec((w.shape[0] // n_steps, w.shape[1]), lambda i, j, k: (i * nj + j, 0))
                  for w in to_round]
    out_specs = [pl.BlockSpec((tm, tn), lambda i, j, k: (i, j)) for _ in out_dtypes]
    out_shape = [jax.ShapeDtypeStruct((M, n_cols), dt) for dt in out_dtypes]
    scratch = [pltpu.VMEM((tm, tn), jnp.float32)] if nk > 1 else []
    res = pl.pallas_call(
        functools.partial(_mm_kernel, nk=nk, n_extra=len(rows), n_out=len(out_dtypes), n_round=len(to_round),
                          epilogue=epilogue),
        grid=(M // tm, nj, nk),
        in_specs=in_specs + slab_specs, out_specs=out_specs + slab_specs,
        out_shape=out_shape + [jax.ShapeDtypeStruct(w.shape, jnp.bfloat16) for w in to_round],
        scratch_shapes=scratch,
        compiler_params=_cparams(("arbitrary" if to_round else "parallel", "arbitrary" if to_round else "parallel",
                                  "arbitrary")), name=name,
    )(a, b, *[r for r, _ in rows], *to_round)
    return res


def _mm_bias_kernel(a_ref, b_ref, bias_ref, o_ref):
    @pl.when(pl.program_id(2) == 0)
    def _():
        o_ref[...] = jnp.broadcast_to(bias_ref[...], o_ref.shape)

    o_ref[...] += jnp.dot(a_ref[...], b_ref[...], preferred_element_type=jnp.float32)


def _matmul_bias(a, b, bias, *, name, tm, tn, tk):
    M, K = a.shape
    N = b.shape[1]
    tm, tn, tk = min(tm, M), min(tn, N), min(tk, K)
    assert M % tm == 0 and N % tn == 0 and K % tk == 0
    return pl.pallas_call(
        _mm_bias_kernel,
        grid=(M // tm, N // tn, K // tk),
        in_specs=[pl.BlockSpec((tm, tk), lambda i, j, k: (i, k)),
                  pl.BlockSpec((tk, tn), lambda i, j, k: (k, j)),
                  pl.BlockSpec((1, tn), lambda i, j, k: (0, j))],
        out_specs=pl.BlockSpec((tm, tn), lambda i, j, k: (i, j)),
        out_shape=jax.ShapeDtypeStruct((M, N), jnp.float32),
        compiler_params=_cparams(("parallel", "parallel", "arbitrary")), name=name,
    )(a, b, bias)


def _epi_plain(acc, extra, outs):
    outs[0][...] = acc.astype(outs[0].dtype)


def _sigmoid(x):
    return 0.5 * jnp.tanh(0.5 * x) + 0.5


def _epi_gate(acc, extra, outs):
    outs[0][...] = _sigmoid(acc + extra[0][...]).astype(outs[0].dtype)


def _epi_relu2(acc, extra, outs):
    h = jnp.maximum(acc + extra[0][...], 0.0)
    outs[0][...] = (h * h).astype(outs[0].dtype)


def _merge_kernel(at_ref, pa_ref, h_ref, pl_ref, ga_ref, gb_ref, o_ref):
    pa = jnp.dot(at_ref[...], pa_ref[...], preferred_element_type=jnp.float32)
    out = ga_ref[...].astype(jnp.float32) * pa
    pb = jnp.dot(h_ref[...], pl_ref[...], preferred_element_type=jnp.float32)
    o_ref[...] = (out + gb_ref[...].astype(jnp.float32) * pb).astype(o_ref.dtype)


def _merge(attn, p_attn, h, p_lru, gates, *, tm=1024, tn=512):
    M, K1 = attn.shape
    K2 = h.shape[1]
    N = p_attn.shape[1]
    tm, tn = min(tm, M), min(tn, N)
    gb_off = N // tn
    return pl.pallas_call(
        _merge_kernel,
        grid=(M // tm, N // tn),
        in_specs=[
            pl.BlockSpec((tm, K1), lambda i, j: (i, 0)),
            pl.BlockSpec((K1, tn), lambda i, j: (0, j)),
            pl.BlockSpec((tm, K2), lambda i, j: (i, 0)),
            pl.BlockSpec((K2, tn), lambda i, j: (0, j)),
            pl.BlockSpec((tm, tn), lambda i, j: (i, j)),
            pl.BlockSpec((tm, tn), lambda i, j: (i, j + gb_off)),
        ],
        out_specs=pl.BlockSpec((tm, tn), lambda i, j: (i, j)),
        out_shape=jax.ShapeDtypeStruct((M, N), jnp.bfloat16),
        compiler_params=_cparams(("parallel", "arbitrary")), name="merge",
    )(attn, p_attn, h, p_lru, gates, gates)


def _ln_rows(u, g_ref, b_ref, outs):
    mu = jnp.mean(u, axis=-1, keepdims=True)
    c = u - mu
    var = jnp.mean(c * c, axis=-1, keepdims=True)
    y = c * lax.rsqrt(var + LN_EPS) * g_ref[...] + b_ref[...]
    for o in outs:
        o[...] = y.astype(o.dtype)


def _res_ln_kernel(u_ref, res_ref, g_ref, b_ref, *outs, alpha):
    _ln_rows(alpha * res_ref[...] + u_ref[...], g_ref, b_ref, outs)


def _residual_layer_norm(u, res, g, b, out_dtypes, *, alpha, tr=256):
    M, D = u.shape
    tr = min(tr, M)
    rows = pl.BlockSpec((tr, D), lambda i: (i, 0))
    vec = pl.BlockSpec((1, D), lambda i: (0, 0))
    return pl.pallas_call(
        functools.partial(_res_ln_kernel, alpha=alpha),
        grid=(M // tr,),
        in_specs=[rows, rows, vec, vec],
        out_specs=[rows for _ in out_dtypes],
        out_shape=[jax.ShapeDtypeStruct((M, D), dt) for dt in out_dtypes],
        compiler_params=_cparams(("parallel",)), name="layer_norm",
    )(u, res, g, b)


def _proj_ln_kernel(a_ref, w_ref, res_ref, g_ref, b_ref, *outs, alpha):
    u = alpha * res_ref[...] + jnp.dot(a_ref[...], w_ref[...], preferred_element_type=jnp.float32)
    _ln_rows(u, g_ref, b_ref, outs)


def _proj_layer_norm(a, w, res, g, b, out_dtypes, *, alpha, tm=128):
    M, K = a.shape
    N = w.shape[1]
    tm = min(tm, M)
    rows = lambda n: pl.BlockSpec((tm, n), lambda i: (i, 0))
    fixed = lambda shape, **kw: pl.BlockSpec(shape, lambda i: (0, 0), **kw)
    return pl.pallas_call(
        functools.partial(_proj_ln_kernel, alpha=alpha),
        grid=(M // tm,),
        in_specs=[rows(K), fixed((K, N), pipeline_mode=pl.Buffered(1)), rows(N), fixed((1, N)), fixed((1, N))],
        out_specs=[rows(N) for _ in out_dtypes],
        out_shape=[jax.ShapeDtypeStruct((M, N), dt) for dt in out_dtypes],
        compiler_params=_cparams(("parallel",)), name="out_proj_ln",
    )(a, w, res, g, b)


def _band_block(q_ref, k_ref, v_ref, bias_ref, *, base, seq, q0, interior, scale, emit):
    if interior:
        case, kstart = 0, q0 - HALF
    else:
        case = jnp.where(q0 == 0, 1, jnp.where(q0 == seq - QBLK, 2, 0))
        kstart = jnp.clip(q0 - HALF, 0, seq - KWIN)
    q = q_ref[pl.ds(pl.multiple_of(base + q0, QBLK), QBLK), :]
    kw = k_ref[pl.ds(pl.multiple_of(base + kstart, HALF), KWIN), :]
    vw = v_ref[pl.ds(pl.multiple_of(base + kstart, HALF), KWIN), :]
    s = lax.dot_general(q, kw, (((1,), (1,)), ((), ())), preferred_element_type=jnp.float32)
    s = s * (scale * LOG2_E) + bias_ref[case]
    m = jnp.max(s, axis=-1, keepdims=True)
    e = jnp.exp2(s - m)
    den = jnp.sum(e, axis=-1, keepdims=True)
    o = jnp.dot(e.astype(jnp.bfloat16), vw, preferred_element_type=jnp.float32) / den
    emit(o, m * LN_2 + jnp.log(den))


def _attn_kernel(slope_ref, q_ref, k_ref, v_ref, out_ref,
                 qf_ref, kf_ref, vf_ref, qd_ref, kd_ref, vd_ref, o_mid_ref, l_mid_ref, o_far_ref, l_far_ref,
                 bias_ref, *, scale):
    S = q_ref.shape[0]
    slope = slope_ref[pl.program_id(0)]
    col = lax.broadcasted_iota(jnp.int32, (QBLK, KWIN), 1)
    row = lax.broadcasted_iota(jnp.int32, (QBLK, KWIN), 0)
    delta = col - row

    def set_bias(dil):
        for case, off in enumerate((-HALF, 0, -2 * HALF)):
            dist = jnp.abs(delta + off)
            bias_ref[case] = jnp.where(dist <= HALF, dist.astype(jnp.float32) * (-slope * dil * LOG2_E), -jnp.inf)

    n_widen = 2
    piece = S // n_widen

    def widen(i, _):
        rows = pl.ds(pl.multiple_of(i * piece, piece), piece)
        qf_ref[rows, :] = q_ref[rows, :].astype(jnp.float32)
        kf_ref[rows, :] = k_ref[rows, :].astype(jnp.float32)
        vf_ref[rows, :] = v_ref[rows, :].astype(jnp.float32)
        return 0

    lax.fori_loop(0, n_widen, widen, 0)

    for dil, o_ref, l_ref in ((DILATIONS[2], o_far_ref, l_far_ref), (DILATIONS[1], o_mid_ref, l_mid_ref)):
        L = S // dil
        group = dil // ATT_PASSES[dil]
        set_bias(dil)

        def residues(i, _, dil=dil, L=L, group=group, o_ref=o_ref, l_ref=l_ref):
            for g in range(group):
                r = i * group + g
                gather, rows = pl.ds(r, L, stride=dil), pl.ds(pl.multiple_of(r * L, L), L)
                qd_ref[rows, :] = qf_ref[gather, :].astype(jnp.bfloat16)
                kd_ref[rows, :] = kf_ref[gather, :].astype(jnp.bfloat16)
                vd_ref[rows, :] = vf_ref[gather, :].astype(jnp.bfloat16)
                for q0 in range(0, L, QBLK):
                    def emit(o, lse, scatter=pl.ds(q0 * dil + r, QBLK, stride=dil)):
                        o_ref[scatter, :] = o
                        l_ref[scatter, :] = jnp.broadcast_to(lse, (QBLK, HEAD_DIM))

                    _band_block(qd_ref, kd_ref, vd_ref, bias_ref, base=r * L, seq=L, q0=q0,
                                interior=0 < q0 < L - QBLK, scale=scale, emit=emit)
            return 0

        lax.fori_loop(0, ATT_PASSES[dil], residues, 0)

    set_bias(DILATIONS[0])
    n_blk = S // QBLK // ATT_PASSES[1]

    def near(i, _):
        for j in range(n_blk):
            q0 = (i * n_blk + j) * QBLK

            def emit_mixed(o_near, l_near, rows=pl.ds(pl.multiple_of(q0, QBLK), QBLK)):
                l_mid, l_far = l_mid_ref[rows, :], l_far_ref[rows, :]
                m = jnp.maximum(jnp.maximum(l_mid, l_far), l_near)
                w_near, w_mid, w_far = jnp.exp(l_near - m), jnp.exp(l_mid - m), jnp.exp(l_far - m)
                mixed = w_near * o_near + w_mid * o_mid_ref[rows, :] + w_far * o_far_ref[rows, :]
                out_ref[rows, :] = (mixed / (w_near + w_mid + w_far)).astype(out_ref.dtype)

            _band_block(q_ref, k_ref, v_ref, bias_ref, base=0, seq=S, q0=q0, interior=0 < j < n_blk - 1,
                        scale=scale, emit=emit_mixed)
        return 0

    lax.fori_loop(0, ATT_PASSES[1], near, 0)


def _dilated_attention(qkv, slopes):
    S = qkv.shape[0]
    assert DILATIONS[0] == 1 and S % (DILATIONS[2] * KWIN) == 0 and S % (ATT_PASSES[1] * QBLK) == 0
    f32, bf16 = jnp.float32, jnp.bfloat16
    head_col = lambda part: pl.BlockSpec((S, HEAD_DIM), lambda h: (0, part * N_HEADS + h))
    return pl.pallas_call(
        functools.partial(_attn_kernel, scale=HEAD_DIM ** -0.5),
        grid=(N_HEADS,),
        in_specs=[pl.BlockSpec(memory_space=pltpu.SMEM), head_col(0), head_col(1), head_col(2)],
        out_specs=pl.BlockSpec((S, HEAD_DIM), lambda h: (0, h)),
        out_shape=jax.ShapeDtypeStruct((S, ATTN_WIDTH), bf16),
        scratch_shapes=[pltpu.VMEM((S, HEAD_DIM), f32)] * 3
        + [pltpu.VMEM((S, HEAD_DIM), bf16)] * 3
        + [pltpu.VMEM((S, HEAD_DIM), f32)] * 4
        + [pltpu.VMEM((3, QBLK, KWIN), f32)],
        compiler_params=_cparams(("parallel",)), name="dilated_attention",
    )(slopes, qkv, qkv, qkv)


def _scan_local(a_slices, b_slices):
    hs, ps = [], []
    h = p = None
    for a, b in zip(a_slices, b_slices):
        h, p = (b, a) if h is None else (a * h + b, a * p)
        hs.append(h)
        ps.append(p)
    return hs, ps


def _slabs(ref, rows=None):
    parts = [ref[s] if rows is None else ref[s, rows, :] for s in range(ref.shape[0])]
    return jnp.concatenate(parts, axis=1) if len(parts) > 1 else parts[0]


def _to_slabs(ref, value, rows=None):
    for s in range(ref.shape[0]):
        piece = value[:, s * LANES:(s + 1) * LANES]
        if rows is None:
            ref[s] = piece
        else:
            ref[s, rows, :] = piece


def _lru_direction(x_ref, xp_ref, xn_ref, cw_ref, cb_ref, wa_ref, wx_ref, ba_ref, bx_ref, lam_ref,
                   carry_ref, ext_ref, t1a_ref, t1b_ref, e1_ref, g_ref, hs_ref, *, tb, n_tb, reverse):
    T, C = x_ref.shape
    n1 = T // SUBLANES
    n2 = n1 // SUBLANES
    strided = lambda k, n: pl.ds(k, n, stride=SUBLANES)
    _to_slabs(ext_ref, jnp.where(tb > 0, xp_ref[...], 0.0), pl.ds(0, SUBLANES))
    _to_slabs(ext_ref, x_ref[...], pl.ds(SUBLANES, T))
    _to_slabs(ext_ref, jnp.where(tb < n_tb - 1, xn_ref[...], 0.0), pl.ds(SUBLANES + T, SUBLANES))
    tap = {o: _slabs(ext_ref, strided(SUBLANES + o, n1)) for o in range(-CONV_LEFT, SUBLANES + 4 - CONV_LEFT - 1)}
    xk = []
    for k in range(SUBLANES):
        xck = cb_ref[...]
        for j in range(4):
            xck = xck + cw_ref[j:j + 1, :] * tap[k + j - CONV_LEFT]
        xk.append(xck)
    xk = jnp.concatenate(xk, axis=0)
    xkb = xk.astype(jnp.bfloat16)
    ga, gx = [], []
    for n in range(C // LRU_BLOCK):
        blk = xkb[:, n * LRU_BLOCK:(n + 1) * LRU_BLOCK]
        ga.append(jnp.dot(blk, wa_ref[n], preferred_element_type=jnp.float32))
        gx.append(jnp.dot(blk, wx_ref[n], preferred_element_type=jnp.float32))
    ga = jnp.concatenate(ga, axis=1) if len(ga) > 1 else ga[0]
    gx = jnp.concatenate(gx, axis=1) if len(gx) > 1 else gx[0]
    t_r = jnp.tanh(0.5 * ga + 0.5 * ba_ref[...])
    t_i = jnp.tanh(0.5 * gx + 0.5 * bx_ref[...])
    nlam = -lam_ref[...]
    softplus = jnp.maximum(nlam, 0.0) + jnp.log1p(jnp.exp(-jnp.abs(nlam)))
    half_rate = (-0.5 * LRU_C * LOG2_E) * softplus
    a = jnp.exp2(half_rate * t_r + half_rate)
    one_m_a2 = 1.0 - a * a
    root = one_m_a2 * lax.rsqrt(jnp.maximum(one_m_a2, F32_TINY))
    b = root * ((0.5 * t_i + 0.5) * xk)

    order = list(range(SUBLANES))[::-1] if reverse else list(range(SUBLANES))
    hs0, ps0 = _scan_local([a[k * n1:(k + 1) * n1] for k in order], [b[k * n1:(k + 1) * n1] for k in order])
    _to_slabs(t1a_ref, ps0[-1])
    _to_slabs(t1b_ref, hs0[-1])
    hs1, ps1 = _scan_local([_slabs(t1a_ref, strided(k, n2)) for k in order],
                           [_slabs(t1b_ref, strided(k, n2)) for k in order])
    grp_a, grp_b = ps1[-1], hs1[-1]
    carry = carry_ref[...]
    for g in (range(n2 - 1, -1, -1) if reverse else range(n2)):
        g_ref[g:g + 1, :] = carry
        carry = grp_a[g:g + 1, :] * carry + grp_b[g:g + 1, :]
    carry_ref[...] = carry
    enter_grp = g_ref[...]
    for j, k in enumerate(order):
        enter = enter_grp if j == 0 else hs1[j - 1] + ps1[j - 1] * enter_grp
        _to_slabs(e1_ref, enter, strided(k, n2))
    enter_tile = _slabs(e1_ref)
    for j, k in enumerate(order):
        _to_slabs(hs_ref, hs0[j] + ps0[j] * enter_tile, strided(k, n1))
    return _slabs(hs_ref)


def _lru_kernel(xf_ref, xfp_ref, xfn_ref, xb_ref, xbp_ref, xbn_ref, cw_ref, cb_ref,
                waf_ref, wxf_ref, baf_ref, bxf_ref, lamf_ref,
                wab_ref, wxb_ref, bab_ref, bxb_ref, lamb_ref,
                *rest, n_tb, n_round):
    w_in_refs, h_ref, w_out_refs = rest[:n_round], rest[n_round], rest[n_round + 1:2 * n_round + 1]
    carry_f, carry_b, *scan_scratch = rest[2 * n_round + 1:]
    s = pl.program_id(1)
    T = xf_ref.shape[0]
    for w_in_ref, w_out_ref in zip(w_in_refs, w_out_refs):
        w_out_ref[...] = w_in_ref[...].astype(w_out_ref.dtype)

    @pl.when(s == 0)
    def _():
        carry_f[...] = jnp.zeros_like(carry_f)
        carry_b[...] = jnp.zeros_like(carry_b)
        h_ref[...] = jnp.zeros_like(h_ref)

    def emit(rows, h):
        h_ref[rows, :] = (h_ref[rows, :].astype(jnp.float32) + h).astype(h_ref.dtype)

    hf = _lru_direction(xf_ref, xfp_ref, xfn_ref, cw_ref, cb_ref, waf_ref, wxf_ref, baf_ref, bxf_ref,
                        lamf_ref, carry_f, *scan_scratch, tb=s, n_tb=n_tb, reverse=False)
    emit(pl.ds(pl.multiple_of(s * T, T), T), hf)
    sb = n_tb - 1 - s
    hb = _lru_direction(xb_ref, xbp_ref, xbn_ref, cw_ref, cb_ref, wab_ref, wxb_ref, bab_ref, bxb_ref,
                        lamb_ref, carry_b, *scan_scratch, tb=sb, n_tb=n_tb, reverse=True)
    emit(pl.ds(pl.multiple_of(sb * T, T), T), hb)


def _rg_lru(xr, conv_w, conv_b, fwd, bwd, to_round, *, tc=512):
    S, C = xr.shape
    T, tc = LRU_T, min(tc, C)
    assert S % T == 0 and C % tc == 0 and tc % LRU_BLOCK == 0
    n_tb = S // T
    n_steps = (C // tc) * n_tb
    slab_rows = [w.shape[0] // n_steps for w in to_round]
    assert all(w.shape[0] % n_steps == 0 and r % BF16_SUBLANES == 0 for w, r in zip(to_round, slab_rows))
    slab_specs = [pl.BlockSpec((r, w.shape[1]), lambda c, s: (c * n_tb + s, 0)) for w, r in zip(to_round, slab_rows)]
    tpb = T // SUBLANES
    n_halo = S // SUBLANES
    nb = tc // LRU_BLOCK
    n_slab = tc // LANES
    n1 = T // SUBLANES

    def main(tb_of):
        return pl.BlockSpec((T, tc), lambda c, s: (tb_of(s), c))

    def prev(tb_of):
        return pl.BlockSpec((SUBLANES, tc), lambda c, s: (jnp.maximum(tb_of(s) * tpb - 1, 0), c))

    def nxt(tb_of):
        return pl.BlockSpec((SUBLANES, tc), lambda c, s: (jnp.minimum((tb_of(s) + 1) * tpb, n_halo - 1), c))

    fwd_tb = lambda s: s
    bwd_tb = lambda s: n_tb - 1 - s
    row = pl.BlockSpec((1, tc), lambda c, s: (0, c))
    wspec = pl.BlockSpec((nb, LRU_BLOCK, LRU_BLOCK), lambda c, s: (c, 0, 0))
    return pl.pallas_call(
        functools.partial(_lru_kernel, n_tb=n_tb, n_round=len(to_round)),
        grid=(C // tc, n_tb),
        in_specs=[main(fwd_tb), prev(fwd_tb), nxt(fwd_tb), main(bwd_tb), prev(bwd_tb), nxt(bwd_tb),
                  pl.BlockSpec((4, tc), lambda c, s: (0, c)), row,
                  wspec, wspec, row, row, row,
                  wspec, wspec, row, row, row] + slab_specs,
        out_specs=[pl.BlockSpec((S, tc), lambda c, s: (0, c))] + slab_specs,
        out_shape=[jax.ShapeDtypeStruct((S, C), jnp.bfloat16)]
        + [jax.ShapeDtypeStruct(w.shape, jnp.bfloat16) for w in to_round],
        scratch_shapes=[pltpu.VMEM((1, tc), jnp.float32), pltpu.VMEM((1, tc), jnp.float32),
                        pltpu.VMEM((n_slab, T + 2 * SUBLANES, LANES), jnp.float32),
                        pltpu.VMEM((n_slab, n1, LANES), jnp.float32),
                        pltpu.VMEM((n_slab, n1, LANES), jnp.float32),
                        pltpu.VMEM((n_slab, n1, LANES), jnp.float32),
                        pltpu.VMEM((n1 // SUBLANES, tc), jnp.float32),
                        pltpu.VMEM((n_slab, T, LANES), jnp.float32)],
        compiler_params=_cparams(("arbitrary", "arbitrary")), name="rg_lru",
    )(xr, xr, xr, xr, xr, xr, conv_w, conv_b, *fwd, *bwd, *to_round)


def kernel(x, w_in, gate_b, conv_w, conv_b, lru_wa_fwd, lru_ba_fwd, lru_wx_fwd, lru_bx_fwd, lru_lam_fwd, lru_wa_bwd, lru_ba_bwd, lru_wx_bwd, lru_bx_bwd, lru_lam_bwd, p_attn, p_lru, w_out, ln1_g, ln1_b, w_mlp1, b_mlp1, w_mlp2, b_mlp2, ln2_g, ln2_b):
    B, S, D = x.shape
    depth = w_in.shape[0]
    lru_width = conv_w.shape[-1]
    alpha = (2.0 * depth) ** 0.25
    bf16, f32 = jnp.bfloat16, jnp.float32
    slopes = 2.0 ** (-8.0 * jnp.arange(1, N_HEADS + 1, dtype=f32) / N_HEADS)
    row = lambda v: v.reshape(1, -1).astype(f32)

    outs = []
    for bi in range(B):
        xs = x[bi]
        for l in range(depth):
            xb = xs.astype(bf16)
            w_in_l = w_in[l]
            qkv, = _matmul(xb, w_in_l, name="proj_qkv", n_cols=3 * ATTN_WIDTH, col_off=0, epilogue=_epi_plain,
                           out_dtypes=[bf16], tn=F32_WEIGHT_TN)
            xr, = _matmul(xb, w_in_l, name="proj_lru", n_cols=lru_width, col_off=3 * ATTN_WIDTH,
                          epilogue=_epi_plain, out_dtypes=[f32], tn=F32_WEIGHT_TN)
            gates, = _matmul(xb, w_in_l, name="proj_gates", n_cols=2 * D, col_off=3 * ATTN_WIDTH + lru_width,
                             epilogue=_epi_gate, out_dtypes=[bf16], rows=[(row(gate_b[l]), 0)], tn=F32_WEIGHT_TN)

            attn = _dilated_attention(qkv, slopes)

            lru = lambda wa, ba, wx, bx, lam: (wa[l].astype(bf16), wx[l].astype(bf16), row(ba[l]), row(bx[l]),
                                               row(lam[l]))
            h, p_attn_b, p_lru_b, w_out_b, w_mlp1_b = _rg_lru(
                xr, conv_w[l], row(conv_b[l]),
                lru(lru_wa_fwd, lru_ba_fwd, lru_wx_fwd, lru_bx_fwd, lru_lam_fwd),
                lru(lru_wa_bwd, lru_ba_bwd, lru_wx_bwd, lru_bx_bwd, lru_lam_bwd),
                [p_attn[l], p_lru[l], w_out[l], w_mlp1[l]])

            merged = _merge(attn, p_attn_b, h, p_lru_b, gates)
            y, yb = _proj_layer_norm(merged, w_out_b, xs, row(ln1_g[l]), row(ln1_b[l]), [f32, bf16], alpha=alpha)

            hid, w_mlp2_b = _matmul(yb, w_mlp1_b, name="mlp1", n_cols=w_mlp1.shape[-1], col_off=0,
                                    epilogue=_epi_relu2, out_dtypes=[bf16], rows=[(row(b_mlp1[l]), 0)],
                                    to_round=[w_mlp2[l]])
            u2 = _matmul_bias(hid, w_mlp2_b, row(b_mlp2[l]), name="mlp2", tm=1024, tn=1024, tk=4096)
            xs, = _residual_layer_norm(u2, y, row(ln2_g[l]), row(ln2_b[l]), [f32], alpha=alpha)
        outs.append(xs)
    return outs[0][None] if B == 1 else jnp.stack(outs, axis=0)
```

```python
import functools

import jax
import jax.numpy as jnp
from jax import lax
from jax.experimental import pallas as pl
from jax.experimental.pallas import tpu as pltpu

HEAD_DIM = 128
N_HEADS = 16
ATTN_WIDTH = N_HEADS * HEAD_DIM
DILATIONS = (1, 4, 16)
HALF = 64
LRU_BLOCK = 256
LRU_C = 8.0
LN_EPS = 1e-5
CONV_LEFT = 2

V7X_VMEM_LIMIT_BYTES = 56 * 1024 * 1024
SUBLANES = 8
BF16_SUBLANES = 16
LANES = 128

LRU_T = 512
LRU_RADIX = 4
LOG2_E = 1.4426950408889634
LN_2 = 0.6931471805599453
F32_TINY = 1e-37
QBLK = 128
KWIN = QBLK + 2 * HALF
ATT_PASSES = {1: 2, 4: 1, 16: 4}
F32_WEIGHT_TN = 512


def _cparams(sem):
    return pltpu.CompilerParams(dimension_semantics=sem, vmem_limit_bytes=V7X_VMEM_LIMIT_BYTES)


def _mm_kernel(*refs, nk, n_extra, n_out, n_round, epilogue):
    a_ref, b_ref = refs[0], refs[1]
    extra = refs[2:2 + n_extra]
    outs = refs[2 + n_extra + n_round:2 + n_extra + n_round + n_out]
    for src, dst in zip(refs[2 + n_extra:2 + n_extra + n_round], refs[2 + n_extra + n_round + n_out:]):
        dst[...] = src[...].astype(dst.dtype)
    prod = jnp.dot(a_ref[...], b_ref[...].astype(jnp.bfloat16), preferred_element_type=jnp.float32)
    if nk == 1:
        epilogue(prod, extra, outs)
        return
    acc_ref = refs[-1]
    k = pl.program_id(2)

    @pl.when(k == 0)
    def _():
        acc_ref[...] = jnp.zeros_like(acc_ref)

    acc_ref[...] += prod

    @pl.when(k == nk - 1)
    def _():
        epilogue(acc_ref[...], extra, outs)


def _matmul(a, b, *, name, n_cols, col_off, epilogue, out_dtypes, rows=(), to_round=(), tm=1024, tn=1024,
            tk=4096):
    M, K = a.shape
    tm, tn, tk = min(tm, M), min(tn, n_cols), min(tk, K)
    while col_off % tn or any(r_off % tn for _, r_off in rows):
        tn //= 2
    assert M % tm == 0 and n_cols % tn == 0 and K % tk == 0 and tn % LANES == 0
    nk = K // tk
    nj = n_cols // tn
    off = col_off // tn
    in_specs = [pl.BlockSpec((tm, tk), lambda i, j, k: (i, k)),
                pl.BlockSpec((tk, tn), lambda i, j, k: (k, j + off))]
    for r, r_off in rows:
        ro = r_off // tn
        in_specs.append(pl.BlockSpec((1, tn), lambda i, j, k, ro=ro: (0, j + ro)))
    n_steps = (M // tm) * nj
    assert not to_round or nk == 1
    assert all(w.shape[0] % (n_steps * BF16_SUBLANES) == 0 for w in to_round)
    slab_specs = [pl.BlockSpec((w.shape[0] // n_steps, w.shape[1]), lambda i, j, k: (i * nj + j, 0))
                  for w in to_round]
    out_specs = [pl.BlockSpec((tm, tn), lambda i, j, k: (i, j)) for _ in out_dtypes]
    out_shape = [jax.ShapeDtypeStruct((M, n_cols), dt) for dt in out_dtypes]
    scratch = [pltpu.VMEM((tm, tn), jnp.float32)] if nk > 1 else []
    res = pl.pallas_call(
        functools.partial(_mm_kernel, nk=nk, n_extra=len(rows), n_out=len(out_dtypes), n_round=len(to_round),
                          epilogue=epilogue),
        grid=(M // tm, nj, nk),
        in_specs=in_specs + slab_specs, out_specs=out_specs + slab_specs,
        out_shape=out_shape + [jax.ShapeDtypeStruct(w.shape, jnp.bfloat16) for w in to_round],
        scratch_shapes=scratch,
        compiler_params=_cparams(("arbitrary" if to_round else "parallel", "arbitrary" if to_round else "parallel",
                                  "arbitrary")), name=name,
    )(a, b, *[r for r, _ in rows], *to_round)
    return res


def _mm_bias_kernel(a_ref, b_ref, bias_ref, o_ref):
    @pl.when(pl.program_id(2) == 0)
    def _():
        o_ref[...] = jnp.broadcast_to(bias_ref[...], o_ref.shape)

    o_ref[...] += jnp.dot(a_ref[...], b_ref[...], preferred_element_type=jnp.float32)


def _matmul_bias(a, b, bias, *, name, tm, tn, tk):
    M, K = a.shape
    N = b.shape[1]
    tm, tn, tk = min(tm, M), min(tn, N), min(tk, K)
    assert M % tm == 0 and N % tn == 0 and K % tk == 0
    return pl.pallas_call(
        _mm_bias_kernel,
        grid=(M // tm, N // tn, K // tk),
        in_specs=[pl.BlockSpec((tm, tk), lambda i, j, k: (i, k)),
                  pl.BlockSpec((tk, tn), lambda i, j, k: (k, j)),
                  pl.BlockSpec((1, tn), lambda i, j, k: (0, j))],
        out_specs=pl.BlockSpec((tm, tn), lambda i, j, k: (i, j)),
        out_shape=jax.ShapeDtypeStruct((M, N), jnp.float32),
        compiler_params=_cparams(("parallel", "parallel", "arbitrary")), name=name,
    )(a, b, bias)


def _epi_plain(acc, extra, outs):
    outs[0][...] = acc.astype(outs[0].dtype)


def _sigmoid(x):
    return 0.5 * jnp.tanh(0.5 * x) + 0.5


def _epi_gate(acc, extra, outs):
    outs[0][...] = _sigmoid(acc + extra[0][...]).astype(outs[0].dtype)


def _epi_relu2(acc, extra, outs):
    h = jnp.maximum(acc + extra[0][...], 0.0)
    outs[0][...] = (h * h).astype(outs[0].dtype)


def _merge_kernel(at_ref, pa_ref, h_ref, pl_ref, ga_ref, gb_ref, o_ref):
    pa = jnp.dot(at_ref[...], pa_ref[...], preferred_element_type=jnp.float32)
    out = ga_ref[...].astype(jnp.float32) * pa
    pb = jnp.dot(h_ref[...], pl_ref[...], preferred_element_type=jnp.float32)
    o_ref[...] = (out + gb_ref[...].astype(jnp.float32) * pb).astype(o_ref.dtype)


def _merge(attn, p_attn, h, p_lru, gates, *, tm=1024, tn=512):
    M, K1 = attn.shape
    K2 = h.shape[1]
    N = p_attn.shape[1]
    tm, tn = min(tm, M), min(tn, N)
    gb_off = N // tn
    return pl.pallas_call(
        _merge_kernel,
        grid=(M // tm, N // tn),
        in_specs=[
            pl.BlockSpec((tm, K1), lambda i, j: (i, 0)),
            pl.BlockSpec((K1, tn), lambda i, j: (0, j)),
            pl.BlockSpec((tm, K2), lambda i, j: (i, 0)),
            pl.BlockSpec((K2, tn), lambda i, j: (0, j)),
            pl.BlockSpec((tm, tn), lambda i, j: (i, j)),
            pl.BlockSpec((tm, tn), lambda i, j: (i, j + gb_off)),
        ],
        out_specs=pl.BlockSpec((tm, tn), lambda i, j: (i, j)),
        out_shape=jax.ShapeDtypeStruct((M, N), jnp.bfloat16),
        compiler_params=_cparams(("parallel", "arbitrary")), name="merge",
    )(attn, p_attn, h, p_lru, gates, gates)


def _ln_rows(u, g_ref, b_ref, outs):
    mu = jnp.mean(u, axis=-1, keepdims=True)
    c = u - mu
    var = jnp.mean(c * c, axis=-1, keepdims=True)
    y = c * lax.rsqrt(var + LN_EPS) * g_ref[...] + b_ref[...]
    for o in outs:
        o[...] = y.astype(o.dtype)


def _res_ln_kernel(u_ref, res_ref, g_ref, b_ref, *outs, alpha):
    _ln_rows(alpha * res_ref[...] + u_ref[...], g_ref, b_ref, outs)


def _residual_layer_norm(u, res, g, b, out_dtypes, *, alpha, tr=256):
    M, D = u.shape
    tr = min(tr, M)
    rows = pl.BlockSpec((tr, D), lambda i: (i, 0))
    vec = pl.BlockSpec((1, D), lambda i: (0, 0))
    return pl.pallas_call(
        functools.partial(_res_ln_kernel, alpha=alpha),
        grid=(M // tr,),
        in_specs=[rows, rows, vec, vec],
        out_specs=[rows for _ in out_dtypes],
        out_shape=[jax.ShapeDtypeStruct((M, D), dt) for dt in out_dtypes],
        compiler_params=_cparams(("parallel",)), name="layer_norm",
    )(u, res, g, b)


def _proj_ln_kernel(a_ref, w_ref, res_ref, g_ref, b_ref, *outs, alpha):
    u = alpha * res_ref[...] + jnp.dot(a_ref[...], w_ref[...], preferred_element_type=jnp.float32)
    _ln_rows(u, g_ref, b_ref, outs)


def _proj_layer_norm(a, w, res, g, b, out_dtypes, *, alpha, tm=128):
    M, K = a.shape
    N = w.shape[1]
    tm = min(tm, M)
    rows = lambda n: pl.BlockSpec((tm, n), lambda i: (i, 0))
    fixed = lambda shape, **kw: pl.BlockSpec(shape, lambda i: (0, 0), **kw)
    return pl.pallas_call(
        functools.partial(_proj_ln_kernel, alpha=alpha),
        grid=(M // tm,),
        in_specs=[rows(K), fixed((K, N), pipeline_mode=pl.Buffered(1)), rows(N), fixed((1, N)), fixed((1, N))],
        out_specs=[rows(N) for _ in out_dtypes],
        out_shape=[jax.ShapeDtypeStruct((M, N), dt) for dt in out_dtypes],
        compiler_params=_cparams(("parallel",)), name="out_proj_ln",
    )(a, w, res, g, b)


def _band_block(q_ref, k_ref, v_ref, bias_ref, *, base, seq, q0, interior, scale, emit):
    if interior:
        case, kstart = 0, q0 - HALF
    else:
        case = jnp.where(q0 == 0, 1, jnp.where(q0 == seq - QBLK, 2, 0))
        kstart = jnp.clip(q0 - HALF, 0, seq - KWIN)
    q = q_ref[pl.ds(pl.multiple_of(base + q0, QBLK), QBLK), :]
    kw = k_ref[pl.ds(pl.multiple_of(base + kstart, HALF), KWIN), :]
    vw = v_ref[pl.ds(pl.multiple_of(base + kstart, HALF), KWIN), :]
    s = lax.dot_general(q, kw, (((1,), (1,)), ((), ())), preferred_element_type=jnp.float32)
    s = s * (scale * LOG2_E) + bias_ref[case]
    m = jnp.max(s, axis=-1, keepdims=True)
    e = jnp.exp2(s - m)
    den = jnp.sum(e, axis=-1, keepdims=True)
    o = jnp.dot(e.astype(jnp.bfloat16), vw, preferred_element_type=jnp.float32) / den
    emit(o, m * LN_2 + jnp.log(den))


def _attn_kernel(slope_ref, q_ref, k_ref, v_ref, out_ref,
                 qf_ref, kf_ref, vf_ref, qd_ref, kd_ref, vd_ref, o_mid_ref, l_mid_ref, o_far_ref, l_far_ref,
                 bias_ref, *, scale):
    S = q_ref.shape[0]
    slope = slope_ref[pl.program_id(0)]
    col = lax.broadcasted_iota(jnp.int32, (QBLK, KWIN), 1)
    row = lax.broadcasted_iota(jnp.int32, (QBLK, KWIN), 0)
    delta = col - row

    def set_bias(dil):
        for case, off in enumerate((-HALF, 0, -2 * HALF)):
            dist = jnp.abs(delta + off)
            bias_ref[case] = jnp.where(dist <= HALF, dist.astype(jnp.float32) * (-slope * dil * LOG2_E), -jnp.inf)

    n_widen = 2
    piece = S // n_widen

    def widen(i, _):
        rows = pl.ds(pl.multiple_of(i * piece, piece), piece)
        qf_ref[rows, :] = q_ref[rows, :].astype(jnp.float32)
        kf_ref[rows, :] = k_ref[rows, :].astype(jnp.float32)
        vf_ref[rows, :] = v_ref[rows, :].astype(jnp.float32)
        return 0

    lax.fori_loop(0, n_widen, widen, 0)

    for dil, o_ref, l_ref in ((DILATIONS[2], o_far_ref, l_far_ref), (DILATIONS[1], o_mid_ref, l_mid_ref)):
        L = S // dil
        group = dil // ATT_PASSES[dil]
        set_bias(dil)

        def residues(i, _, dil=dil, L=L, group=group, o_ref=o_ref, l_ref=l_ref):
            for g in range(group):
                r = i * group + g
                gather, rows = pl.ds(r, L, stride=dil), pl.ds(pl.multiple_of(r * L, L), L)
                qd_ref[rows, :] = qf_ref[gather, :].astype(jnp.bfloat16)
                kd_ref[rows, :] = kf_ref[gather, :].astype(jnp.bfloat16)
                vd_ref[rows, :] = vf_ref[gather, :].astype(jnp.bfloat16)
                for q0 in range(0, L, QBLK):
                    def emit(o, lse, scatter=pl.ds(q0 * dil + r, QBLK, stride=dil)):
                        o_ref[scatter, :] = o
                        l_ref[scatter, :] = jnp.broadcast_to(lse, (QBLK, HEAD_DIM))

                    _band_block(qd_ref, kd_ref, vd_ref, bias_ref, base=r * L, seq=L, q0=q0,
                                interior=0 < q0 < L - QBLK, scale=scale, emit=emit)
            return 0

        lax.fori_loop(0, ATT_PASSES[dil], residues, 0)

    set_bias(DILATIONS[0])
    n_blk = S // QBLK // ATT_PASSES[1]

    def near(i, _):
        for j in range(n_blk):
            q0 = (i * n_blk + j) * QBLK

            def emit_mixed(o_near, l_near, rows=pl.ds(pl.multiple_of(q0, QBLK), QBLK)):
                l_mid, l_far = l_mid_ref[rows, :], l_far_ref[rows, :]
                m = jnp.maximum(jnp.maximum(l_mid, l_far), l_near)
                w_near, w_mid, w_far = jnp.exp(l_near - m), jnp.exp(l_mid - m), jnp.exp(l_far - m)
                mixed = w_near * o_near + w_mid * o_mid_ref[rows, :] + w_far * o_far_ref[rows, :]
                out_ref[rows, :] = (mixed / (w_near + w_mid + w_far)).astype(out_ref.dtype)

            _band_block(q_ref, k_ref, v_ref, bias_ref, base=0, seq=S, q0=q0, interior=0 < j < n_blk - 1,
                        scale=scale, emit=emit_mixed)
        return 0

    lax.fori_loop(0, ATT_PASSES[1], near, 0)


def _dilated_attention(qkv, slopes):
    S = qkv.shape[0]
    assert DILATIONS[0] == 1 and S % (DILATIONS[2] * KWIN) == 0 and S % (ATT_PASSES[1] * QBLK) == 0
    f32, bf16 = jnp.float32, jnp.bfloat16
    head_col = lambda part: pl.BlockSpec((S, HEAD_DIM), lambda h: (0, part * N_HEADS + h))
    return pl.pallas_call(
        functools.partial(_attn_kernel, scale=HEAD_DIM ** -0.5),
        grid=(N_HEADS,),
        in_specs=[pl.BlockSpec(memory_space=pltpu.SMEM), head_col(0), head_col(1), head_col(2)],
        out_specs=pl.BlockSpec((S, HEAD_DIM), lambda h: (0, h)),
        out_shape=jax.ShapeDtypeStruct((S, ATTN_WIDTH), bf16),
        scratch_shapes=[pltpu.VMEM((S, HEAD_DIM), f32)] * 3
        + [pltpu.VMEM((S, HEAD_DIM), bf16)] * 3
        + [pltpu.VMEM((S, HEAD_DIM), f32)] * 4
        + [pltpu.VMEM((3, QBLK, KWIN), f32)],
        compiler_params=_cparams(("parallel",)), name="dilated_attention",
    )(slopes, qkv, qkv, qkv)


def _scan_local(a_slices, b_slices):
    hs, ps = [], []
    h = p = None
    for a, b in zip(a_slices, b_slices):
        h, p = (b, a) if h is None else (a * h + b, a * p)
        hs.append(h)
        ps.append(p)
    return hs, ps


def _slabs(ref, rows=None):
    parts = [ref[s] if rows is None else ref[s, rows, :] for s in range(ref.shape[0])]
    return jnp.concatenate(parts, axis=1) if len(parts) > 1 else parts[0]


def _to_slabs(ref, value, rows=None):
    for s in range(ref.shape[0]):
        piece = value[:, s * LANES:(s + 1) * LANES]
        if rows is None:
            ref[s] = piece
        else:
            ref[s, rows, :] = piece


def _lru_direction(x_ref, xp_ref, xn_ref, cw_ref, cb_ref, wa_ref, wx_ref, ba_ref, bx_ref, lam_ref,
                   carry_ref, ext_ref, t1a_ref, t1b_ref, e1_ref, g_ref, hs_ref, *, tb, n_tb, reverse):
    T, C = x_ref.shape
    R = LRU_RADIX
    n1 = T // R
    n2 = n1 // R
    strided = lambda k, n: pl.ds(k, n, stride=R)
    _to_slabs(ext_ref, jnp.where(tb > 0, xp_ref[...], 0.0), pl.ds(0, SUBLANES))
    _to_slabs(ext_ref, x_ref[...], pl.ds(SUBLANES, T))
    _to_slabs(ext_ref, jnp.where(tb < n_tb - 1, xn_ref[...], 0.0), pl.ds(SUBLANES + T, SUBLANES))
    tap = {o: _slabs(ext_ref, strided(SUBLANES + o, n1)) for o in range(-CONV_LEFT, R + 4 - CONV_LEFT - 1)}
    xk = []
    for k in range(R):
        xck = cb_ref[...]
        for j in range(4):
            xck = xck + cw_ref[j:j + 1, :] * tap[k + j - CONV_LEFT]
        xk.append(xck)
    xk = jnp.concatenate(xk, axis=0)
    xkb = xk.astype(jnp.bfloat16)
    ga, gx = [], []
    for n in range(C // LRU_BLOCK):
        blk = xkb[:, n * LRU_BLOCK:(n + 1) * LRU_BLOCK]
        ga.append(jnp.dot(blk, wa_ref[n], preferred_element_type=jnp.float32))
        gx.append(jnp.dot(blk, wx_ref[n], preferred_element_type=jnp.float32))
    ga = jnp.concatenate(ga, axis=1) if len(ga) > 1 else ga[0]
    gx = jnp.concatenate(gx, axis=1) if len(gx) > 1 else gx[0]
    t_r = jnp.tanh(0.5 * ga + 0.5 * ba_ref[...])
    t_i = jnp.tanh(0.5 * gx + 0.5 * bx_ref[...])
    nlam = -lam_ref[...]
    softplus = jnp.maximum(nlam, 0.0) + jnp.log1p(jnp.exp(-jnp.abs(nlam)))
    half_rate = (-0.5 * LRU_C * LOG2_E) * softplus
    a = jnp.exp2(half_rate * t_r + half_rate)
    one_m_a2 = 1.0 - a * a
    root = one_m_a2 * lax.rsqrt(jnp.maximum(one_m_a2, F32_TINY))
    b = root * ((0.5 * t_i + 0.5) * xk)

    order = list(range(R))[::-1] if reverse else list(range(R))
    hs0, ps0 = _scan_local([a[k * n1:(k + 1) * n1] for k in order], [b[k * n1:(k + 1) * n1] for k in order])
    _to_slabs(t1a_ref, ps0[-1])
    _to_slabs(t1b_ref, hs0[-1])
    hs1, ps1 = _scan_local([_slabs(t1a_ref, strided(k, n2)) for k in order],
                           [_slabs(t1b_ref, strided(k, n2)) for k in order])
    grp_a, grp_b = ps1[-1], hs1[-1]
    carry = carry_ref[...]
    for g in (range(n2 - 1, -1, -1) if reverse else range(n2)):
        g_ref[g:g + 1, :] = carry
        carry = grp_a[g:g + 1, :] * carry + grp_b[g:g + 1, :]
    carry_ref[...] = carry
    enter_grp = g_ref[...]
    for j, k in enumerate(order):
        enter = enter_grp if j == 0 else hs1[j - 1] + ps1[j - 1] * enter_grp
        _to_slabs(e1_ref, enter, strided(k, n2))
    enter_tile = _slabs(e1_ref)
    for j, k in enumerate(order):
        _to_slabs(hs_ref, hs0[j] + ps0[j] * enter_tile, strided(k, n1))
    return _slabs(hs_ref)


def _lru_kernel(xf_ref, xfp_ref, xfn_ref, xb_ref, xbp_ref, xbn_ref, cw_ref, cb_ref,
                waf_ref, wxf_ref, baf_ref, bxf_ref, lamf_ref,
                wab_ref, wxb_ref, bab_ref, bxb_ref, lamb_ref,
                *rest, n_tb, n_round):
    w_in_refs, h_ref, w_out_refs = rest[:n_round], rest[n_round], rest[n_round + 1:2 * n_round + 1]
    carry_f, carry_b, *scan_scratch = rest[2 * n_round + 1:]
    s = pl.program_id(1)
    T = xf_ref.shape[0]
    for w_in_ref, w_out_ref in zip(w_in_refs, w_out_refs):
        w_out_ref[...] = w_in_ref[...].astype(w_out_ref.dtype)

    @pl.when(s == 0)
    def _():
        carry_f[...] = jnp.zeros_like(carry_f)
        carry_b[...] = jnp.zeros_like(carry_b)
        h_ref[...] = jnp.zeros_like(h_ref)

    def emit(rows, h):
        h_ref[rows, :] = (h_ref[rows, :].astype(jnp.float32) + h).astype(h_ref.dtype)

    hf = _lru_direction(xf_ref, xfp_ref, xfn_ref, cw_ref, cb_ref, waf_ref, wxf_ref, baf_ref, bxf_ref,
                        lamf_ref, carry_f, *scan_scratch, tb=s, n_tb=n_tb, reverse=False)
    emit(pl.ds(pl.multiple_of(s * T, T), T), hf)
    sb = n_tb - 1 - s
    hb = _lru_direction(xb_ref, xbp_ref, xbn_ref, cw_ref, cb_ref, wab_ref, wxb_ref, bab_ref, bxb_ref,
                        lamb_ref, carry_b, *scan_scratch, tb=sb, n_tb=n_tb, reverse=True)
    emit(pl.ds(pl.multiple_of(sb * T, T), T), hb)


def _rg_lru(xr, conv_w, conv_b, fwd, bwd, to_round, *, tc=512):
    S, C = xr.shape
    T, tc = LRU_T, min(tc, C)
    assert S % T == 0 and C % tc == 0 and tc % LRU_BLOCK == 0
    n_tb = S // T
    n_steps = (C // tc) * n_tb
    slab_rows = [w.shape[0] // n_steps for w in to_round]
    assert all(w.shape[0] % n_steps == 0 and r % BF16_SUBLANES == 0 for w, r in zip(to_round, slab_rows))
    slab_specs = [pl.BlockSpec((r, w.shape[1]), lambda c, s: (c * n_tb + s, 0)) for w, r in zip(to_round, slab_rows)]
    tpb = T // SUBLANES
    n_halo = S // SUBLANES
    nb = tc // LRU_BLOCK
    n_slab = tc // LANES
    n1 = T // LRU_RADIX

    def main(tb_of):
        return pl.BlockSpec((T, tc), lambda c, s: (tb_of(s), c))

    def prev(tb_of):
        return pl.BlockSpec((SUBLANES, tc), lambda c, s: (jnp.maximum(tb_of(s) * tpb - 1, 0), c))

    def nxt(tb_of):
        return pl.BlockSpec((SUBLANES, tc), lambda c, s: (jnp.minimum((tb_of(s) + 1) * tpb, n_halo - 1), c))

    fwd_tb = lambda s: s
    bwd_tb = lambda s: n_tb - 1 - s
    row = pl.BlockSpec((1, tc), lambda c, s: (0, c))
    wspec = pl.BlockSpec((nb, LRU_BLOCK, LRU_BLOCK), lambda c, s: (c, 0, 0))
    return pl.pallas_call(
        functools.partial(_lru_kernel, n_tb=n_tb, n_round=len(to_round)),
        grid=(C // tc, n_tb),
        in_specs=[main(fwd_tb), prev(fwd_tb), nxt(fwd_tb), main(bwd_tb), prev(bwd_tb), nxt(bwd_tb),
                  pl.BlockSpec((4, tc), lambda c, s: (0, c)), row,
                  wspec, wspec, row, row, row,
                  wspec, wspec, row, row, row] + slab_specs,
        out_specs=[pl.BlockSpec((S, tc), lambda c, s: (0, c))] + slab_specs,
        out_shape=[jax.ShapeDtypeStruct((S, C), jnp.bfloat16)]
        + [jax.ShapeDtypeStruct(w.shape, jnp.bfloat16) for w in to_round],
        scratch_shapes=[pltpu.VMEM((1, tc), jnp.float32), pltpu.VMEM((1, tc), jnp.float32),
                        pltpu.VMEM((n_slab, T + 2 * SUBLANES, LANES), jnp.float32),
                        pltpu.VMEM((n_slab, n1, LANES), jnp.float32),
                        pltpu.VMEM((n_slab, n1, LANES), jnp.float32),
                        pltpu.VMEM((n_slab, n1, LANES), jnp.float32),
                        pltpu.VMEM((n1 // LRU_RADIX, tc), jnp.float32),
                        pltpu.VMEM((n_slab, T, LANES), jnp.float32)],
        compiler_params=_cparams(("arbitrary", "arbitrary")), name="rg_lru",
    )(xr, xr, xr, xr, xr, xr, conv_w, conv_b, *fwd, *bwd, *to_round)


def kernel(x, w_in, gate_b, conv_w, conv_b, lru_wa_fwd, lru_ba_fwd, lru_wx_fwd, lru_bx_fwd, lru_lam_fwd, lru_wa_bwd, lru_ba_bwd, lru_wx_bwd, lru_bx_bwd, lru_lam_bwd, p_attn, p_lru, w_out, ln1_g, ln1_b, w_mlp1, b_mlp1, w_mlp2, b_mlp2, ln2_g, ln2_b):
    B, S, D = x.shape
    depth = w_in.shape[0]
    lru_width = conv_w.shape[-1]
    alpha = (2.0 * depth) ** 0.25
    bf16, f32 = jnp.bfloat16, jnp.float32
    slopes = 2.0 ** (-8.0 * jnp.arange(1, N_HEADS + 1, dtype=f32) / N_HEADS)
    row = lambda v: v.reshape(1, -1).astype(f32)

    outs = []
    for bi in range(B):
        xs = x[bi]
        for l in range(depth):
            xb = xs.astype(bf16)
            w_in_l = w_in[l]
            qkv, = _matmul(xb, w_in_l, name="proj_qkv", n_cols=3 * ATTN_WIDTH, col_off=0, epilogue=_epi_plain,
                           out_dtypes=[bf16], tn=F32_WEIGHT_TN)
            xr, = _matmul(xb, w_in_l, name="proj_lru", n_cols=lru_width, col_off=3 * ATTN_WIDTH,
                          epilogue=_epi_plain, out_dtypes=[f32], tn=F32_WEIGHT_TN)
            gates, = _matmul(xb, w_in_l, name="proj_gates", n_cols=2 * D, col_off=3 * ATTN_WIDTH + lru_width,
                             epilogue=_epi_gate, out_dtypes=[bf16], rows=[(row(gate_b[l]), 0)], tn=F32_WEIGHT_TN)

            attn = _dilated_attention(qkv, slopes)

            lru = lambda wa, ba, wx, bx, lam: (wa[l].astype(bf16), wx[l].astype(bf16), row(ba[l]), row(bx[l]),
                                               row(lam[l]))
            h, p_attn_b, p_lru_b, w_out_b, w_mlp1_b = _rg_lru(
                xr, conv_w[l], row(conv_b[l]),
                lru(lru_wa_fwd, lru_ba_fwd, lru_wx_fwd, lru_bx_fwd, lru_lam_fwd),
                lru(lru_wa_bwd, lru_ba_bwd, lru_wx_bwd, lru_bx_bwd, lru_lam_bwd),
                [p_attn[l], p_lru[l], w_out[l], w_mlp1[l]])

            merged = _merge(attn, p_attn_b, h, p_lru_b, gates)
            y, yb = _proj_layer_norm(merged, w_out_b, xs, row(ln1_g[l]), row(ln1_b[l]), [f32, bf16], alpha=alpha)

            hid, w_mlp2_b = _matmul(yb, w_mlp1_b, name="mlp1", n_cols=w_mlp1.shape[-1], col_off=0,
                                    epilogue=_epi_relu2, out_dtypes=[bf16], rows=[(row(b_mlp1[l]), 0)],
                                    to_round=[w_mlp2[l]])
            u2 = _matmul_bias(hid, w_mlp2_b, row(b_mlp2[l]), name="mlp2", tm=1024, tn=1024, tk=4096)
            xs, = _residual_layer_norm(u2, y, row(ln2_g[l]), row(ln2_b[l]), [f32], alpha=alpha)
        outs.append(xs)
    return outs[0][None] if B == 1 else jnp.stack(outs, axis=0)
```

```python
import functools

import jax
import jax.numpy as jnp
from jax import lax
from jax.experimental import pallas as pl
from jax.experimental.pallas import tpu as pltpu

HEAD_DIM = 128
N_HEADS = 16
ATTN_WIDTH = N_HEADS * HEAD_DIM
DILATIONS = (1, 4, 16)
HALF = 64
LRU_BLOCK = 256
LRU_C = 8.0
LN_EPS = 1e-5
CONV_LEFT = 2

V7X_VMEM_LIMIT_BYTES = 56 * 1024 * 1024
ATTN_VMEM_LIMIT_BYTES = 58 * 1024 * 1024
SUBLANES = 8
BF16_SUBLANES = 16
LANES = 128

LRU_T = 512
LRU_RADIX = 4
LOG2_E = 1.4426950408889634
LN_2 = 0.6931471805599453
F32_TINY = 1e-37
QBLK = 128
KWIN = QBLK + 2 * HALF
ATT_PASSES = {1: 2, 4: 1, 16: 4}
F32_WEIGHT_TN = 512


def _cparams(sem, vmem_limit_bytes=V7X_VMEM_LIMIT_BYTES):
    return pltpu.CompilerParams(dimension_semantics=sem, vmem_limit_bytes=vmem_limit_bytes)


def _mm_kernel(*refs, nk, n_extra, n_out, n_round, epilogue):
    a_ref, b_ref = refs[0], refs[1]
    extra = refs[2:2 + n_extra]
    outs = refs[2 + n_extra + n_round:2 + n_extra + n_round + n_out]
    for src, dst in zip(refs[2 + n_extra:2 + n_extra + n_round], refs[2 + n_extra + n_round + n_out:]):
        dst[...] = src[...].astype(dst.dtype)
    prod = jnp.dot(a_ref[...], b_ref[...].astype(jnp.bfloat16), preferred_element_type=jnp.float32)
    if nk == 1:
        epilogue(prod, extra, outs)
        return
    acc_ref = refs[-1]
    k = pl.program_id(2)

    @pl.when(k == 0)
    def _():
        acc_ref[...] = jnp.zeros_like(acc_ref)

    acc_ref[...] += prod

    @pl.when(k == nk - 1)
    def _():
        epilogue(acc_ref[...], extra, outs)


def _matmul(a, b, *, name, n_cols, col_off, epilogue, out_dtypes, rows=(), to_round=(), tm=1024, tn=1024,
            tk=4096):
    M, K = a.shape
    tm, tn, tk = min(tm, M), min(tn, n_cols), min(tk, K)
    while col_off % tn or any(r_off % tn for _, r_off in rows):
        tn //= 2
    assert M % tm == 0 and n_cols % tn == 0 and K % tk == 0 and tn % LANES == 0
    nk = K // tk
    nj = n_cols // tn
    off = col_off // tn
    in_specs = [pl.BlockSpec((tm, tk), lambda i, j, k: (i, k)),
                pl.BlockSpec((tk, tn), lambda i, j, k: (k, j + off))]
    for r, r_off in rows:
        ro = r_off // tn
        in_specs.append(pl.BlockSpec((1, tn), lambda i, j, k, ro=ro: (0, j + ro)))
    n_steps = (M // tm) * nj
    assert not to_round or nk == 1
    assert all(w.shape[0] % (n_steps * BF16_SUBLANES) == 0 for w in to_round)
    slab_specs = [pl.BlockSpec((w.shape[0] // n_steps, w.shape[1]), lambda i, j, k: (i * nj + j, 0))
                  for w in to_round]
    out_specs = [pl.BlockSpec((tm, tn), lambda i, j, k: (i, j)) for _ in out_dtypes]
    out_shape = [jax.ShapeDtypeStruct((M, n_cols), dt) for dt in out_dtypes]
    scratch = [pltpu.VMEM((tm, tn), jnp.float32)] if nk > 1 else []
    res = pl.pallas_call(
        functools.partial(_mm_kernel, nk=nk, n_extra=len(rows), n_out=len(out_dtypes), n_round=len(to_round),
                          epilogue=epilogue),
        grid=(M // tm, nj, nk),
        in_specs=in_specs + slab_specs, out_specs=out_specs + slab_specs,
        out_shape=out_shape + [jax.ShapeDtypeStruct(w.shape, jnp.bfloat16) for w in to_round],
        scratch_shapes=scratch,
        compiler_params=_cparams(("arbitrary" if to_round else "parallel", "arbitrary" if to_round else "parallel",
                                  "arbitrary")), name=name,
    )(a, b, *[r for r, _ in rows], *to_round)
    return res


def _mm_bias_kernel(a_ref, b_ref, bias_ref, o_ref):
    @pl.when(pl.program_id(2) == 0)
    def _():
        o_ref[...] = jnp.broadcast_to(bias_ref[...], o_ref.shape)

    o_ref[...] += jnp.dot(a_ref[...], b_ref[...], preferred_element_type=jnp.float32)


def _matmul_bias(a, b, bias, *, name, tm, tn, tk):
    M, K = a.shape
    N = b.shape[1]
    tm, tn, tk = min(tm, M), min(tn, N), min(tk, K)
    assert M % tm == 0 and N % tn == 0 and K % tk == 0
    return pl.pallas_call(
        _mm_bias_kernel,
        grid=(M // tm, N // tn, K // tk),
        in_specs=[pl.BlockSpec((tm, tk), lambda i, j, k: (i, k)),
                  pl.BlockSpec((tk, tn), lambda i, j, k: (k, j)),
                  pl.BlockSpec((1, tn), lambda i, j, k: (0, j))],
        out_specs=pl.BlockSpec((tm, tn), lambda i, j, k: (i, j)),
        out_shape=jax.ShapeDtypeStruct((M, N), jnp.float32),
        compiler_params=_cparams(("parallel", "parallel", "arbitrary")), name=name,
    )(a, b, bias)


def _epi_plain(acc, extra, outs):
    outs[0][...] = acc.astype(outs[0].dtype)


def _sigmoid(x):
    return 0.5 * jnp.tanh(0.5 * x) + 0.5


def _epi_gate(acc, extra, outs):
    outs[0][...] = _sigmoid(acc + extra[0][...]).astype(outs[0].dtype)


def _epi_relu2(acc, extra, outs):
    h = jnp.maximum(acc + extra[0][...], 0.0)
    outs[0][...] = (h * h).astype(outs[0].dtype)


def _merge_kernel(at_ref, pa_ref, h_ref, pl_ref, ga_ref, gb_ref, o_ref):
    pa = jnp.dot(at_ref[...], pa_ref[...], preferred_element_type=jnp.float32)
    out = ga_ref[...].astype(jnp.float32) * pa
    pb = jnp.dot(h_ref[...], pl_ref[...], preferred_element_type=jnp.float32)
    o_ref[...] = (out + gb_ref[...].astype(jnp.float32) * pb).astype(o_ref.dtype)


def _merge(attn, p_attn, h, p_lru, gates, *, tm=1024, tn=512):
    M, K1 = attn.shape
    K2 = h.shape[1]
    N = p_attn.shape[1]
    tm, tn = min(tm, M), min(tn, N)
    gb_off = N // tn
    return pl.pallas_call(
        _merge_kernel,
        grid=(M // tm, N // tn),
        in_specs=[
            pl.BlockSpec((tm, K1), lambda i, j: (i, 0)),
            pl.BlockSpec((K1, tn), lambda i, j: (0, j)),
            pl.BlockSpec((tm, K2), lambda i, j: (i, 0)),
            pl.BlockSpec((K2, tn), lambda i, j: (0, j)),
            pl.BlockSpec((tm, tn), lambda i, j: (i, j)),
            pl.BlockSpec((tm, tn), lambda i, j: (i, j + gb_off)),
        ],
        out_specs=pl.BlockSpec((tm, tn), lambda i, j: (i, j)),
        out_shape=jax.ShapeDtypeStruct((M, N), jnp.bfloat16),
        compiler_params=_cparams(("parallel", "arbitrary")), name="merge",
    )(attn, p_attn, h, p_lru, gates, gates)


def _ln_rows(u, g_ref, b_ref, outs):
    mu = jnp.mean(u, axis=-1, keepdims=True)
    c = u - mu
    var = jnp.mean(c * c, axis=-1, keepdims=True)
    y = c * lax.rsqrt(var + LN_EPS) * g_ref[...] + b_ref[...]
    for o in outs:
        o[...] = y.astype(o.dtype)


def _res_ln_kernel(u_ref, res_ref, g_ref, b_ref, *outs, alpha):
    _ln_rows(alpha * res_ref[...] + u_ref[...], g_ref, b_ref, outs)


def _residual_layer_norm(u, res, g, b, out_dtypes, *, alpha, tr=256):
    M, D = u.shape
    tr = min(tr, M)
    rows = pl.BlockSpec((tr, D), lambda i: (i, 0))
    vec = pl.BlockSpec((1, D), lambda i: (0, 0))
    return pl.pallas_call(
        functools.partial(_res_ln_kernel, alpha=alpha),
        grid=(M // tr,),
        in_specs=[rows, rows, vec, vec],
        out_specs=[rows for _ in out_dtypes],
        out_shape=[jax.ShapeDtypeStruct((M, D), dt) for dt in out_dtypes],
        compiler_params=_cparams(("parallel",)), name="layer_norm",
    )(u, res, g, b)


def _proj_ln_kernel(a_ref, w_ref, res_ref, g_ref, b_ref, *outs, alpha):
    u = alpha * res_ref[...] + jnp.dot(a_ref[...], w_ref[...], preferred_element_type=jnp.float32)
    _ln_rows(u, g_ref, b_ref, outs)


def _proj_layer_norm(a, w, res, g, b, out_dtypes, *, alpha, tm=128):
    M, K = a.shape
    N = w.shape[1]
    tm = min(tm, M)
    rows = lambda n: pl.BlockSpec((tm, n), lambda i: (i, 0))
    fixed = lambda shape, **kw: pl.BlockSpec(shape, lambda i: (0, 0), **kw)
    return pl.pallas_call(
        functools.partial(_proj_ln_kernel, alpha=alpha),
        grid=(M // tm,),
        in_specs=[rows(K), fixed((K, N), pipeline_mode=pl.Buffered(1)), rows(N), fixed((1, N)), fixed((1, N))],
        out_specs=[rows(N) for _ in out_dtypes],
        out_shape=[jax.ShapeDtypeStruct((M, N), dt) for dt in out_dtypes],
        compiler_params=_cparams(("parallel",)), name="out_proj_ln",
    )(a, w, res, g, b)


def _band_block(q_ref, k_ref, v_ref, bias_ref, *, base, seq, q0, interior, scale, emit):
    if interior:
        case, kstart = 0, q0 - HALF
    else:
        case = jnp.where(q0 == 0, 1, jnp.where(q0 == seq - QBLK, 2, 0))
        kstart = jnp.clip(q0 - HALF, 0, seq - KWIN)
    q = q_ref[pl.ds(pl.multiple_of(base + q0, QBLK), QBLK), :]
    kw = k_ref[pl.ds(pl.multiple_of(base + kstart, HALF), KWIN), :]
    vw = v_ref[pl.ds(pl.multiple_of(base + kstart, HALF), KWIN), :]
    s = lax.dot_general(q, kw, (((1,), (1,)), ((), ())), preferred_element_type=jnp.float32)
    s = s * (scale * LOG2_E) + bias_ref[case]
    m = jnp.max(s, axis=-1, keepdims=True)
    e = jnp.exp2(s - m)
    den = jnp.sum(e, axis=-1, keepdims=True)
    o = jnp.dot(e.astype(jnp.bfloat16), vw, preferred_element_type=jnp.float32) / den
    emit(o, m * LN_2 + jnp.log(den))


def _attn_kernel(slope_ref, q_ref, k_ref, v_ref, out_ref,
                 qf_ref, kf_ref, vf_ref, qd_ref, kd_ref, vd_ref, o_mid_ref, l_mid_ref, o_far_ref, l_far_ref,
                 bias_ref, *, scale):
    S = q_ref.shape[0]
    slope = slope_ref[pl.program_id(0)]
    col = lax.broadcasted_iota(jnp.int32, (QBLK, KWIN), 1)
    row = lax.broadcasted_iota(jnp.int32, (QBLK, KWIN), 0)
    delta = col - row

    def set_bias(dil):
        for case, off in enumerate((-HALF, 0, -2 * HALF)):
            dist = jnp.abs(delta + off)
            bias_ref[case] = jnp.where(dist <= HALF, dist.astype(jnp.float32) * (-slope * dil * LOG2_E), -jnp.inf)

    n_widen = 2
    piece = S // n_widen

    def widen(i, _):
        rows = pl.ds(pl.multiple_of(i * piece, piece), piece)
        qf_ref[rows, :] = q_ref[rows, :].astype(jnp.float32)
        kf_ref[rows, :] = k_ref[rows, :].astype(jnp.float32)
        vf_ref[rows, :] = v_ref[rows, :].astype(jnp.float32)
        return 0

    lax.fori_loop(0, n_widen, widen, 0)

    mid, far = DILATIONS[1], DILATIONS[2]
    l4 = S // mid
    q4_ref, k4_ref, v4_ref = o_mid_ref, l_mid_ref, qf_ref
    for src, dst in ((qf_ref, q4_ref), (kf_ref, k4_ref), (vf_ref, v4_ref)):
        for c in range(mid):
            dst[pl.ds(c * l4, l4), :] = src[pl.ds(c, l4, stride=mid), :]

    for dil, o_ref, l_ref in ((far, o_far_ref, l_far_ref), (mid, o_mid_ref, l_mid_ref)):
        L = S // dil
        group = dil // ATT_PASSES[dil]
        set_bias(dil)

        def regroup(r, dil=dil, L=L):
            rows = pl.ds(pl.multiple_of(r * L, L), L)
            gather = rows if dil == mid else pl.ds((r % mid) * l4 + r // mid, L, stride=far // mid)
            qd_ref[rows, :] = q4_ref[gather, :].astype(jnp.bfloat16)
            kd_ref[rows, :] = k4_ref[gather, :].astype(jnp.bfloat16)
            vd_ref[rows, :] = v4_ref[gather, :].astype(jnp.bfloat16)

        def residues(i, _, dil=dil, L=L, group=group, o_ref=o_ref, l_ref=l_ref, regroup=regroup):
            if dil == mid:
                for g in range(group):
                    regroup(i * group + g)
            for g in range(group):
                r = i * group + g
                if dil != mid:
                    regroup(r)
                for q0 in range(0, L, QBLK):
                    def emit(o, lse, scatter=pl.ds(q0 * dil + r, QBLK, stride=dil)):
                        o_ref[scatter, :] = o
                        l_ref[scatter, :] = jnp.broadcast_to(lse, (QBLK, HEAD_DIM))

                    _band_block(qd_ref, kd_ref, vd_ref, bias_ref, base=r * L, seq=L, q0=q0,
                                interior=0 < q0 < L - QBLK, scale=scale, emit=emit)
            return 0

        lax.fori_loop(0, ATT_PASSES[dil], residues, 0)

    set_bias(DILATIONS[0])
    n_blk = S // QBLK // ATT_PASSES[1]

    def near(i, _):
        for j in range(n_blk):
            q0 = (i * n_blk + j) * QBLK

            def emit_mixed(o_near, l_near, rows=pl.ds(pl.multiple_of(q0, QBLK), QBLK)):
                l_mid, l_far = l_mid_ref[rows, :], l_far_ref[rows, :]
                m = jnp.maximum(jnp.maximum(l_mid, l_far), l_near)
                w_near, w_mid, w_far = jnp.exp(l_near - m), jnp.exp(l_mid - m), jnp.exp(l_far - m)
                mixed = w_near * o_near + w_mid * o_mid_ref[rows, :] + w_far * o_far_ref[rows, :]
                out_ref[rows, :] = (mixed / (w_near + w_mid + w_far)).astype(out_ref.dtype)

            _band_block(q_ref, k_ref, v_ref, bias_ref, base=0, seq=S, q0=q0, interior=0 < j < n_blk - 1,
                        scale=scale, emit=emit_mixed)
        return 0

    lax.fori_loop(0, ATT_PASSES[1], near, 0)


def _dilated_attention(qkv, slopes):
    S = qkv.shape[0]
    assert DILATIONS[0] == 1 and S % (DILATIONS[2] * KWIN) == 0 and S % (ATT_PASSES[1] * QBLK) == 0
    assert DILATIONS[2] == DILATIONS[1] ** 2 and ATT_PASSES[DILATIONS[1]] == 1
    f32, bf16 = jnp.float32, jnp.bfloat16
    head_col = lambda part: pl.BlockSpec((S, HEAD_DIM), lambda h: (0, part * N_HEADS + h))
    return pl.pallas_call(
        functools.partial(_attn_kernel, scale=HEAD_DIM ** -0.5),
        grid=(N_HEADS,),
        in_specs=[pl.BlockSpec(memory_space=pltpu.SMEM), head_col(0), head_col(1), head_col(2)],
        out_specs=pl.BlockSpec((S, HEAD_DIM), lambda h: (0, h)),
        out_shape=jax.ShapeDtypeStruct((S, ATTN_WIDTH), bf16),
        scratch_shapes=[pltpu.VMEM((S, HEAD_DIM), f32)] * 3
        + [pltpu.VMEM((S, HEAD_DIM), bf16)] * 3
        + [pltpu.VMEM((S, HEAD_DIM), f32)] * 4
        + [pltpu.VMEM((3, QBLK, KWIN), f32)],
        compiler_params=_cparams(("parallel",), ATTN_VMEM_LIMIT_BYTES), name="dilated_attention",
    )(slopes, qkv, qkv, qkv)


def _scan_local(a_slices, b_slices):
    hs, ps = [], []
    h = p = None
    for a, b in zip(a_slices, b_slices):
        h, p = (b, a) if h is None else (a * h + b, a * p)
        hs.append(h)
        ps.append(p)
    return hs, ps


def _slabs(ref, rows=None):
    parts = [ref[s] if rows is None else ref[s, rows, :] for s in range(ref.shape[0])]
    return jnp.concatenate(parts, axis=1) if len(parts) > 1 else parts[0]


def _to_slabs(ref, value, rows=None):
    for s in range(ref.shape[0]):
        piece = value[:, s * LANES:(s + 1) * LANES]
        if rows is None:
            ref[s] = piece
        else:
            ref[s, rows, :] = piece


def _lru_direction(x_ref, xp_ref, xn_ref, cw_ref, cb_ref, wa_ref, wx_ref, ba_ref, bx_ref, lam_ref,
                   carry_ref, ext_ref, t1a_ref, t1b_ref, e1_ref, g_ref, hs_ref, *, tb, n_tb, reverse):
    T, C = x_ref.shape
    R = LRU_RADIX
    n1 = T // R
    n2 = n1 // R
    strided = lambda k, n: pl.ds(k, n, stride=R)
    _to_slabs(ext_ref, jnp.where(tb > 0, xp_ref[...], 0.0), pl.ds(0, SUBLANES))
    _to_slabs(ext_ref, x_ref[...], pl.ds(SUBLANES, T))
    _to_slabs(ext_ref, jnp.where(tb < n_tb - 1, xn_ref[...], 0.0), pl.ds(SUBLANES + T, SUBLANES))
    tap = {o: _slabs(ext_ref, strided(SUBLANES + o, n1)) for o in range(-CONV_LEFT, R + 4 - CONV_LEFT - 1)}
    xk = []
    for k in range(R):
        xck = cb_ref[...]
        for j in range(4):
            xck = xck + cw_ref[j:j + 1, :] * tap[k + j - CONV_LEFT]
        xk.append(xck)
    xk = jnp.concatenate(xk, axis=0)
    xkb = xk.astype(jnp.bfloat16)
    ga, gx = [], []
    for n in range(C // LRU_BLOCK):
        blk = xkb[:, n * LRU_BLOCK:(n + 1) * LRU_BLOCK]
        ga.append(jnp.dot(blk, wa_ref[n], preferred_element_type=jnp.float32))
        gx.append(jnp.dot(blk, wx_ref[n], preferred_element_type=jnp.float32))
    ga = jnp.concatenate(ga, axis=1) if len(ga) > 1 else ga[0]
    gx = jnp.concatenate(gx, axis=1) if len(gx) > 1 else gx[0]
    t_r = jnp.tanh(0.5 * ga + 0.5 * ba_ref[...])
    t_i = jnp.tanh(0.5 * gx + 0.5 * bx_ref[...])
    nlam = -lam_ref[...]
    softplus = jnp.maximum(nlam, 0.0) + jnp.log1p(jnp.exp(-jnp.abs(nlam)))
    half_rate = (-0.5 * LRU_C * LOG2_E) * softplus
    a = jnp.exp2(half_rate * t_r + half_rate)
    one_m_a2 = 1.0 - a * a
    root = one_m_a2 * lax.rsqrt(jnp.maximum(one_m_a2, F32_TINY))
    b = root * ((0.5 * t_i + 0.5) * xk)

    order = list(range(R))[::-1] if reverse else list(range(R))
    hs0, ps0 = _scan_local([a[k * n1:(k + 1) * n1] for k in order], [b[k * n1:(k + 1) * n1] for k in order])
    _to_slabs(t1a_ref, ps0[-1])
    _to_slabs(t1b_ref, hs0[-1])
    hs1, ps1 = _scan_local([_slabs(t1a_ref, strided(k, n2)) for k in order],
                           [_slabs(t1b_ref, strided(k, n2)) for k in order])
    grp_a, grp_b = ps1[-1], hs1[-1]
    carry = carry_ref[...]
    for g in (range(n2 - 1, -1, -1) if reverse else range(n2)):
        g_ref[g:g + 1, :] = carry
        carry = grp_a[g:g + 1, :] * carry + grp_b[g:g + 1, :]
    carry_ref[...] = carry
    enter_grp = g_ref[...]
    for j, k in enumerate(order):
        enter = enter_grp if j == 0 else hs1[j - 1] + ps1[j - 1] * enter_grp
        _to_slabs(e1_ref, enter, strided(k, n2))
    enter_tile = _slabs(e1_ref)
    for j, k in enumerate(order):
        _to_slabs(hs_ref, hs0[j] + ps0[j] * enter_tile, strided(k, n1))
    return _slabs(hs_ref)


def _lru_kernel(xf_ref, xfp_ref, xfn_ref, xb_ref, xbp_ref, xbn_ref, cw_ref, cb_ref,
                waf_ref, wxf_ref, baf_ref, bxf_ref, lamf_ref,
                wab_ref, wxb_ref, bab_ref, bxb_ref, lamb_ref,
                *rest, n_tb, n_round):
    w_in_refs, h_ref, w_out_refs = rest[:n_round], rest[n_round], rest[n_round + 1:2 * n_round + 1]
    carry_f, carry_b, *scan_scratch = rest[2 * n_round + 1:]
    s = pl.program_id(1)
    T = xf_ref.shape[0]
    for w_in_ref, w_out_ref in zip(w_in_refs, w_out_refs):
        w_out_ref[...] = w_in_ref[...].astype(w_out_ref.dtype)

    @pl.when(s == 0)
    def _():
        carry_f[...] = jnp.zeros_like(carry_f)
        carry_b[...] = jnp.zeros_like(carry_b)
        h_ref[...] = jnp.zeros_like(h_ref)

    def emit(rows, h):
        h_ref[rows, :] = (h_ref[rows, :].astype(jnp.float32) + h).astype(h_ref.dtype)

    hf = _lru_direction(xf_ref, xfp_ref, xfn_ref, cw_ref, cb_ref, waf_ref, wxf_ref, baf_ref, bxf_ref,
                        lamf_ref, carry_f, *scan_scratch, tb=s, n_tb=n_tb, reverse=False)
    emit(pl.ds(pl.multiple_of(s * T, T), T), hf)
    sb = n_tb - 1 - s
    hb = _lru_direction(xb_ref, xbp_ref, xbn_ref, cw_ref, cb_ref, wab_ref, wxb_ref, bab_ref, bxb_ref,
                        lamb_ref, carry_b, *scan_scratch, tb=sb, n_tb=n_tb, reverse=True)
    emit(pl.ds(pl.multiple_of(sb * T, T), T), hb)


def _rg_lru(xr, conv_w, conv_b, fwd, bwd, to_round, *, tc=512):
    S, C = xr.shape
    T, tc = LRU_T, min(tc, C)
    assert S % T == 0 and C % tc == 0 and tc % LRU_BLOCK == 0
    n_tb = S // T
    n_steps = (C // tc) * n_tb
    slab_rows = [w.shape[0] // n_steps for w in to_round]
    assert all(w.shape[0] % n_steps == 0 and r % BF16_SUBLANES == 0 for w, r in zip(to_round, slab_rows))
    slab_specs = [pl.BlockSpec((r, w.shape[1]), lambda c, s: (c * n_tb + s, 0)) for w, r in zip(to_round, slab_rows)]
    tpb = T // SUBLANES
    n_halo = S // SUBLANES
    nb = tc // LRU_BLOCK
    n_slab = tc // LANES
    n1 = T // LRU_RADIX

    def main(tb_of):
        return pl.BlockSpec((T, tc), lambda c, s: (tb_of(s), c))

    def prev(tb_of):
        return pl.BlockSpec((SUBLANES, tc), lambda c, s: (jnp.maximum(tb_of(s) * tpb - 1, 0), c))

    def nxt(tb_of):
        return pl.BlockSpec((SUBLANES, tc), lambda c, s: (jnp.minimum((tb_of(s) + 1) * tpb, n_halo - 1), c))

    fwd_tb = lambda s: s
    bwd_tb = lambda s: n_tb - 1 - s
    row = pl.BlockSpec((1, tc), lambda c, s: (0, c))
    wspec = pl.BlockSpec((nb, LRU_BLOCK, LRU_BLOCK), lambda c, s: (c, 0, 0))
    return pl.pallas_call(
        functools.partial(_lru_kernel, n_tb=n_tb, n_round=len(to_round)),
        grid=(C // tc, n_tb),
        in_specs=[main(fwd_tb), prev(fwd_tb), nxt(fwd_tb), main(bwd_tb), prev(bwd_tb), nxt(bwd_tb),
                  pl.BlockSpec((4, tc), lambda c, s: (0, c)), row,
                  wspec, wspec, row, row, row,
                  wspec, wspec, row, row, row] + slab_specs,
        out_specs=[pl.BlockSpec((S, tc), lambda c, s: (0, c))] + slab_specs,
        out_shape=[jax.ShapeDtypeStruct((S, C), jnp.bfloat16)]
        + [jax.ShapeDtypeStruct(w.shape, jnp.bfloat16) for w in to_round],
        scratch_shapes=[pltpu.VMEM((1, tc), jnp.float32), pltpu.VMEM((1, tc), jnp.float32),
                        pltpu.VMEM((n_slab, T + 2 * SUBLANES, LANES), jnp.float32),
                        pltpu.VMEM((n_slab, n1, LANES), jnp.float32),
                        pltpu.VMEM((n_slab, n1, LANES), jnp.float32),
                        pltpu.VMEM((n_slab, n1, LANES), jnp.float32),
                        pltpu.VMEM((n1 // LRU_RADIX, tc), jnp.float32),
                        pltpu.VMEM((n_slab, T, LANES), jnp.float32)],
        compiler_params=_cparams(("arbitrary", "arbitrary")), name="rg_lru",
    )(xr, xr, xr, xr, xr, xr, conv_w, conv_b, *fwd, *bwd, *to_round)


def kernel(x, w_in, gate_b, conv_w, conv_b, lru_wa_fwd, lru_ba_fwd, lru_wx_fwd, lru_bx_fwd, lru_lam_fwd, lru_wa_bwd, lru_ba_bwd, lru_wx_bwd, lru_bx_bwd, lru_lam_bwd, p_attn, p_lru, w_out, ln1_g, ln1_b, w_mlp1, b_mlp1, w_mlp2, b_mlp2, ln2_g, ln2_b):
    B, S, D = x.shape
    depth = w_in.shape[0]
    lru_width = conv_w.shape[-1]
    alpha = (2.0 * depth) ** 0.25
    bf16, f32 = jnp.bfloat16, jnp.float32
    slopes = 2.0 ** (-8.0 * jnp.arange(1, N_HEADS + 1, dtype=f32) / N_HEADS)
    row = lambda v: v.reshape(1, -1).astype(f32)

    outs = []
    for bi in range(B):
        xs = x[bi]
        for l in range(depth):
            xb = xs.astype(bf16)
            w_in_l = w_in[l]
            qkv, = _matmul(xb, w_in_l, name="proj_qkv", n_cols=3 * ATTN_WIDTH, col_off=0, epilogue=_epi_plain,
                           out_dtypes=[bf16], tn=F32_WEIGHT_TN)
            xr, = _matmul(xb, w_in_l, name="proj_lru", n_cols=lru_width, col_off=3 * ATTN_WIDTH,
                          epilogue=_epi_plain, out_dtypes=[f32], tn=F32_WEIGHT_TN)
            gates, = _matmul(xb, w_in_l, name="proj_gates", n_cols=2 * D, col_off=3 * ATTN_WIDTH + lru_width,
                             epilogue=_epi_gate, out_dtypes=[bf16], rows=[(row(gate_b[l]), 0)], tn=F32_WEIGHT_TN)

            attn = _dilated_attention(qkv, slopes)

            lru = lambda wa, ba, wx, bx, lam: (wa[l].astype(bf16), wx[l].astype(bf16), row(ba[l]), row(bx[l]),
                                               row(lam[l]))
            h, p_attn_b, p_lru_b, w_out_b, w_mlp1_b = _rg_lru(
                xr, conv_w[l], row(conv_b[l]),
                lru(lru_wa_fwd, lru_ba_fwd, lru_wx_fwd, lru_bx_fwd, lru_lam_fwd),
                lru(lru_wa_bwd, lru_ba_bwd, lru_wx_bwd, lru_bx_bwd, lru_lam_bwd),
                [p_attn[l], p_lru[l], w_out[l], w_mlp1[l]])

            merged = _merge(attn, p_attn_b, h, p_lru_b, gates)
            y, yb = _proj_layer_norm(merged, w_out_b, xs, row(ln1_g[l]), row(ln1_b[l]), [f32, bf16], alpha=alpha)

            hid, w_mlp2_b = _matmul(yb, w_mlp1_b, name="mlp1", n_cols=w_mlp1.shape[-1], col_off=0,
                                    epilogue=_epi_relu2, out_dtypes=[bf16], rows=[(row(b_mlp1[l]), 0)],
                                    to_round=[w_mlp2[l]])
            u2 = _matmul_bias(hid, w_mlp2_b, row(b_mlp2[l]), name="mlp2", tm=1024, tn=1024, tk=4096)
            xs, = _residual_layer_norm(u2, y, row(ln2_g[l]), row(ln2_b[l]), [f32], alpha=alpha)
        outs.append(xs)
    return outs[0][None] if B == 1 else jnp.stack(outs, axis=0)
```
